```python
import functools
import jax
import jax.numpy as jnp
from jax import lax
import numpy as np

D_MODEL = 1024
BATCH = 8
SEQ = 2048
DEPTH = 1
DEC_BATCH = 128
DEC_SEQ = 4
PAST_LEN = 16384
PAGE_SIZE = 128

D_MIX = D_MODEL
D_ATTN = D_MIX // 2
D_CONV = D_MIX - D_ATTN
HEAD_DIM = 64
N_HEADS = D_ATTN // HEAD_DIM
N_KV_HEADS = 2
GROUP = N_HEADS // N_KV_HEADS
KV_DIM = N_KV_HEADS * HEAD_DIM
WINDOW = 128
ATTN_BLOCK = WINDOW
WIN_BUF = min(WINDOW, PAST_LEN)
CONV_W = 3
IN_SPLITS = (D_ATTN, D_ATTN + KV_DIM, D_ATTN + 2 * KV_DIM,
             D_ATTN + 2 * KV_DIM + D_CONV, D_ATTN + 2 * KV_DIM + 2 * D_CONV)
IN_COLS = D_ATTN + 2 * KV_DIM + 3 * D_CONV
N_ADA = 6
PEER_HEADS = 8
N_KEYS = 128
N_EXPERTS = N_KEYS * N_KEYS
PEER_TOPK = 16
D_KEY = 256
D_HALF = D_KEY // 2
PEER_BLOCK = 256
EPS = 1e-6
NEG_INF = -1e30

kernel_name = 'hymba_conv_swa_peer_step'


def rmsnorm(x, g):
    xf = x.astype(jnp.float32)
    y = xf * lax.rsqrt(jnp.mean(xf * xf, axis=-1, keepdims=True) + EPS)
    return (y * g.astype(jnp.float32)).astype(x.dtype)


def alibi_slopes():
    h = jnp.arange(1, N_HEADS + 1, dtype=jnp.float32)
    return jnp.exp2(-8.0 * h / N_HEADS).reshape(N_KV_HEADS, GROUP)


def attn_core(q, k, v, dist, valid, sinks):
    s = jnp.einsum('nbqkgd,nbskd->nbkgqs', q, k, preferred_element_type=jnp.float32) * (HEAD_DIM ** -0.5)
    s = s - alibi_slopes()[:, :, None, None] * dist[:, None, None]
    s = jnp.where(valid[:, None, None], s, NEG_INF)
    sink = sinks.astype(jnp.float32).reshape(N_KV_HEADS, GROUP)[:, :, None, None]
    m = jnp.maximum(jnp.max(s, axis=-1, keepdims=True), sink)
    p = jnp.exp(s - m)
    w = p / (jnp.sum(p, axis=-1, keepdims=True) + jnp.exp(sink - m))
    o = jnp.einsum('nbkgqs,nbskd->nbqkgd', w, v.astype(jnp.float32))
    return o.astype(q.dtype)


def window_attention_prompt(q, k, v, sinks):
    b, s = q.shape[:2]
    nb = s // ATTN_BLOCK
    qb = q.reshape(b, nb, ATTN_BLOCK, N_KV_HEADS, GROUP, HEAD_DIM)

    def band(t):
        tb = t.reshape(b, nb, ATTN_BLOCK, N_KV_HEADS, HEAD_DIM)
        prev = jnp.pad(tb, ((0, 0), (1, 0), (0, 0), (0, 0), (0, 0)))[:, :-1]
        return jnp.concatenate([prev, tb], axis=2)

    blk = jnp.arange(nb)[:, None, None]
    qpos = blk * ATTN_BLOCK + jnp.arange(ATTN_BLOCK)[None, :, None]
    kpos = (blk - 1) * ATTN_BLOCK + jnp.arange(2 * ATTN_BLOCK)[None, None, :]
    d = qpos - kpos
    valid = (d >= 0) & (d < WINDOW) & (kpos >= 0)
    o = attn_core(qb, band(k), band(v), d.astype(jnp.float32), valid, sinks)
    return o.reshape(b, s, D_ATTN)


def prompt_attend(q, k, v, sinks):
    return window_attention_prompt(q, k, v, sinks), k[:, -WIN_BUF:], v[:, -WIN_BUF:]


def sample_attend(q, k, v, k_buf, v_buf, sinks):
    n, t = q.shape[:2]
    k_all = jnp.concatenate([k_buf.astype(k.dtype), k], axis=1)
    v_all = jnp.concatenate([v_buf.astype(v.dtype), v], axis=1)
    qpos = PAST_LEN + jnp.arange(t)
    kpos = jnp.concatenate([PAST_LEN - WIN_BUF + jnp.arange(WIN_BUF), qpos])
    d = qpos[:, None] - kpos[None, :]
    valid = (d >= 0) & (d < WINDOW) & (kpos[None, :] >= 0)
    o = attn_core(q.reshape(n, 1, t, N_KV_HEADS, GROUP, HEAD_DIM), k_all[:, None], v_all[:, None],
                  d[None].astype(jnp.float32), valid[None], sinks)
    return o.reshape(n, t, D_ATTN), k_all[:, -WIN_BUF:], v_all[:, -WIN_BUF:]


def short_conv(u, prev, w):
    t = u.shape[1]
    up = jnp.concatenate([prev.astype(u.dtype), u], axis=1)
    y = w[0] * up[:, 0:t]
    for j in range(1, CONV_W):
        y = y + w[j] * up[:, j:j + t]
    return y, up[:, -(CONV_W - 1):]


def peer_ffn(h, wq, keys, u_tab, v_tab):
    shape = h.shape
    xt = h.reshape(-1, shape[-1])
    n_tok = xt.shape[0]
    pad = (-n_tok) % PEER_BLOCK
    xb = jnp.pad(xt, ((0, pad), (0, 0))).reshape(-1, PEER_BLOCK, shape[-1])
    n_cand = PEER_TOPK * PEER_TOPK

    def block(xblk):
        q = (xblk @ wq).reshape(PEER_BLOCK, PEER_HEADS, 2, D_HALF)
        sc = jnp.einsum('thpc,hpnc->thpn', q, keys, preferred_element_type=jnp.float32)
        s_top, i_top = lax.top_k(sc, PEER_TOPK)
        cand = (s_top[:, :, 0, :, None] + s_top[:, :, 1, None, :]).reshape(PEER_BLOCK, PEER_HEADS, n_cand)
        cand_idx = (i_top[:, :, 0, :, None] * N_KEYS + i_top[:, :, 1, None, :]).reshape(PEER_BLOCK, PEER_HEADS, n_cand)
        best, pos = lax.top_k(cand, PEER_TOPK)
        experts = jnp.take_along_axis(cand_idx, pos, axis=-1)
        gates = jax.nn.softmax(best, axis=-1)
        act = jax.nn.gelu(jnp.einsum('thkd,td->thk', u_tab[experts], xblk,
                                     preferred_element_type=jnp.float32), approximate=False)
        return jnp.einsum('thk,thkd->td', (gates * act).astype(v_tab.dtype), v_tab[experts])

    y = lax.map(block, xb).reshape(-1, shape[-1])[:n_tok]
    return y.reshape(shape).astype(h.dtype)


def decoder_layer(x, c, conv_prev, attend, norm_mix_g, norm_ffn_g, w_ada, b_ada, w_in, conv_w, w_out,
                  peer_wq, peer_keys, peer_u, peer_v):
    n, t = x.shape[:2]
    mod = jax.nn.silu(c) @ w_ada + b_ada
    sh1, sc1, g1, sh2, sc2, g2 = [m[:, None, :] for m in jnp.split(mod, N_ADA, axis=-1)]
    h = rmsnorm(x, norm_mix_g) * (1 + sc1) + sh1
    q, k, v, bg, cg, hc = jnp.split(h @ w_in, IN_SPLITS, axis=-1)
    z_attn, k_state, v_state = attend(q.reshape(n, t, N_HEADS, HEAD_DIM),
                                      k.reshape(n, t, N_KV_HEADS, HEAD_DIM),
                                      v.reshape(n, t, N_KV_HEADS, HEAD_DIM))
    z_conv, conv_state = short_conv(cg * hc, conv_prev, conv_w)
    z_conv = bg * z_conv
    x = x + g1 * (jnp.concatenate([z_attn, z_conv], axis=-1) @ w_out)
    h2 = rmsnorm(x, norm_ffn_g) * (1 + sc2) + sh2
    x = x + g2 * peer_ffn(h2, peer_wq, peer_keys, peer_u, peer_v)
    return x, k_state, v_state, conv_state


def setup_inputs(seed: int = 0) -> dict:
    key = jax.random.key(seed)
    ks = jax.random.split(key, 20)

    def nrm(k, shape, scale):
        return jax.random.normal(k, shape, jnp.float32) * scale

    return {
        'x_prompt': nrm(ks[0], (BATCH, SEQ, D_MODEL), 1.0),
        'x_sample': nrm(ks[1], (DEC_BATCH, DEC_SEQ, D_MODEL), 1.0),
        'c_prompt': nrm(ks[2], (BATCH, D_MODEL), 1.0),
        'c_sample': nrm(ks[3], (DEC_BATCH, D_MODEL), 1.0),
        'cache_k': nrm(ks[4], (DEPTH, DEC_BATCH, WIN_BUF, N_KV_HEADS, HEAD_DIM), 1.0),
        'cache_v': nrm(ks[5], (DEPTH, DEC_BATCH, WIN_BUF, N_KV_HEADS, HEAD_DIM), 1.0),
        'state_conv': nrm(ks[6], (DEPTH, DEC_BATCH, CONV_W - 1, D_CONV), 1.0),
        'norm_mix_g': 1.0 + nrm(ks[7], (DEPTH, D_MODEL), 0.05),
        'norm_ffn_g': 1.0 + nrm(ks[8], (DEPTH, D_MODEL), 0.05),
        'w_ada': nrm(ks[9], (DEPTH, D_MODEL, N_ADA * D_MODEL), 0.5 * D_MODEL ** -0.5),
        'b_ada': nrm(ks[10], (DEPTH, N_ADA * D_MODEL), 0.01),
        'w_in': nrm(ks[11], (DEPTH, D_MODEL, IN_COLS), D_MODEL ** -0.5),
        'conv_w': nrm(ks[12], (DEPTH, CONV_W, D_CONV), CONV_W ** -0.5),
        'attn_sinks': nrm(ks[13], (DEPTH, N_HEADS), 1.0),
        'w_out': nrm(ks[14], (DEPTH, D_MIX, D_MODEL), D_MIX ** -0.5),
        'peer_wq': nrm(ks[15], (DEPTH, D_MODEL, PEER_HEADS * D_KEY), D_MODEL ** -0.5),
        'peer_keys': nrm(ks[16], (DEPTH, PEER_HEADS, 2, N_KEYS, D_HALF), D_HALF ** -0.5),
        'peer_u': nrm(ks[17], (DEPTH, N_EXPERTS, D_MODEL), D_MODEL ** -0.5),
        'peer_v': nrm(ks[18], (DEPTH, N_EXPERTS, D_MODEL), 1.0),
        'norm_final_g': 1.0 + nrm(ks[19], (D_MODEL,), 0.05),
    }


def reference(x_prompt, x_sample, c_prompt, c_sample, cache_k, cache_v, state_conv,
              norm_mix_g, norm_ffn_g, w_ada, b_ada, w_in, conv_w, attn_sinks, w_out,
              peer_wq, peer_keys, peer_u, peer_v, norm_final_g):
    yp, ys = x_prompt, x_sample
    kp_l, vp_l, cp_l, ks_l, vs_l, cs_l = [], [], [], [], [], []
    for l in range(DEPTH):
        weights = (norm_mix_g[l], norm_ffn_g[l], w_ada[l], b_ada[l], w_in[l], conv_w[l], w_out[l],
                   peer_wq[l], peer_keys[l], peer_u[l], peer_v[l])
        att_p = functools.partial(prompt_attend, sinks=attn_sinks[l])
        att_s = functools.partial(sample_attend, k_buf=cache_k[l], v_buf=cache_v[l], sinks=attn_sinks[l])
        conv0 = jnp.zeros((yp.shape[0], CONV_W - 1, D_CONV), yp.dtype)
        yp, kp, vp, cp = decoder_layer(yp, c_prompt, conv0, att_p, *weights)
        ys, kss, vss, css = decoder_layer(ys, c_sample, state_conv[l], att_s, *weights)
        kp_l.append(kp); vp_l.append(vp); cp_l.append(cp)
        ks_l.append(kss); vs_l.append(vss); cs_l.append(css)
    yp = rmsnorm(yp, norm_final_g)
    ys = rmsnorm(ys, norm_final_g)
    return (yp, ys, jnp.stack(kp_l), jnp.stack(vp_l), jnp.stack(cp_l),
            jnp.stack(ks_l), jnp.stack(vs_l), jnp.stack(cs_l))
```

```python
import functools

import jax
import jax.numpy as jnp
import numpy as np
from jax import lax
from jax.experimental import pallas as pl
from jax.experimental.pallas import tpu as pltpu

F32 = jnp.float32
BF16 = jnp.bfloat16

D_MODEL = 1024
D_ATTN = 512
D_CONV = 512
HEAD_DIM = 64
N_HEADS = 8
N_KV_HEADS = 2
GROUP = 4
KV_DIM = 128
WINDOW = 128
CONV_W = 3
IN_COLS = D_ATTN + 2 * KV_DIM + 3 * D_CONV
COL_K = D_ATTN
COL_V = D_ATTN + KV_DIM
COL_B = D_ATTN + 2 * KV_DIM
COL_C = COL_B + D_CONV
COL_H = COL_C + D_CONV
N_ADA = 6
PEER_HEADS = 8
N_KEYS = 128
N_EXPERTS = N_KEYS * N_KEYS
TOPK = 16
EPS = 1e-6
NEG_INF = -1e30
SQRT_HALF = float(np.sqrt(0.5))

LANES = 128
SUBLANES = 8
MIX_TILE = 512
SAMPLE_SEQS = 16
PEER_TILE = 512
PEER_EBLK = 1024
VMEM_LIMIT = 56 * 1024 * 1024

SLOPES = tuple(float(2.0 ** (-8.0 * (h + 1) / N_HEADS)) for h in range(N_HEADS))


def _rms(x, g):
    return x * lax.rsqrt(jnp.mean(x * x, axis=-1, keepdims=True) + EPS) * g


def _per_group(rows_group, vals):
    out = jnp.full(rows_group.shape, vals[0], F32)
    for g in range(1, len(vals)):
        out = jnp.where(rows_group == g, vals[g], out)
    return out


def _ada_kernel(c_ref, w_ref, b_ref, o_ref):
    s = jax.nn.silu(c_ref[...])
    o_ref[...] = jnp.dot(s.astype(BF16), w_ref[...].astype(BF16),
                         preferred_element_type=F32) + b_ref[...]


def _ada(c, w_ada, b_ada):
    n = c.shape[0]
    return pl.pallas_call(
        _ada_kernel,
        grid=(N_ADA,),
        in_specs=[pl.BlockSpec((n, D_MODEL), lambda j: (0, 0)),
                  pl.BlockSpec((D_MODEL, D_MODEL), lambda j: (0, j)),
                  pl.BlockSpec((1, D_MODEL), lambda j: (0, j))],
        out_specs=pl.BlockSpec((None, n, D_MODEL), lambda j: (j, 0, 0)),
        out_shape=jax.ShapeDtypeStruct((N_ADA, n, D_MODEL), F32),
        name="ada",
    )(c, w_ada, b_ada.reshape(1, -1))


def _sink_softmax(s, sink):
    m = jnp.maximum(jnp.max(s, axis=-1, keepdims=True), sink)
    p = jnp.exp(s - m)
    den = jnp.sum(p, axis=-1, keepdims=True) + jnp.exp(sink - m)
    return p / den


def _mix_prompt_kernel(sink_ref, x_ref, mod_ref, g_ref, win_ref, cw_ref, wout_ref,
                       x1_ref, ko_ref, vo_ref, co_ref,
                       proj, zbuf, ubuf, kprev, vprev):
    j = pl.program_id(1)
    tm = x_ref.shape[0]
    nblk = tm // WINDOW

    @pl.when(j == 0)
    def _():
        kprev[...] = jnp.zeros_like(kprev)
        vprev[...] = jnp.zeros_like(vprev)
        ubuf[0:SUBLANES, :] = jnp.zeros((SUBLANES, D_CONV), F32)

    x = x_ref[...]
    h = _rms(x, g_ref[...]) * (1.0 + mod_ref[1:2, :]) + mod_ref[0:1, :]
    proj[...] = jnp.dot(h.astype(BF16), win_ref[...], preferred_element_type=F32)

    shp = (GROUP * WINDOW, 2 * WINDOW)
    row = lax.broadcasted_iota(jnp.int32, shp, 0)
    col = lax.broadcasted_iota(jnp.int32, shp, 1)
    dist = (row & (WINDOW - 1)) + WINDOW - col
    valid_any = (dist >= 0) & (dist < WINDOW)
    distf = dist.astype(F32)
    grp = lax.broadcasted_iota(jnp.int32, (GROUP * WINDOW, 1), 0) // WINDOW
    first_col = jnp.where(j == 0, WINDOW, 0)

    for kv in range(N_KV_HEADS):
        pen = _per_group(grp, SLOPES[kv * GROUP:(kv + 1) * GROUP]) * distf
        sink = _per_group(grp, [sink_ref[kv * GROUP + g] for g in range(GROUP)])
        ks = slice(kv * HEAD_DIM, (kv + 1) * HEAD_DIM)
        for b in range(nblk):
            rows = slice(b * WINDOW, (b + 1) * WINDOW)
            if b == 0:
                kp, vp = kprev[:, ks], vprev[:, ks]
                valid = valid_any & (col >= first_col)
            else:
                prows = slice((b - 1) * WINDOW, b * WINDOW)
                kp = proj[prows, COL_K + kv * HEAD_DIM:COL_K + (kv + 1) * HEAD_DIM]
                vp = proj[prows, COL_V + kv * HEAD_DIM:COL_V + (kv + 1) * HEAD_DIM]
                valid = valid_any
            kc = proj[rows, COL_K + kv * HEAD_DIM:COL_K + (kv + 1) * HEAD_DIM]
            vc = proj[rows, COL_V + kv * HEAD_DIM:COL_V + (kv + 1) * HEAD_DIM]
            q4 = jnp.concatenate(
                [proj[rows, (kv * GROUP + g) * HEAD_DIM:(kv * GROUP + g + 1) * HEAD_DIM]
                 for g in range(GROUP)], axis=0)
            kcat = jnp.concatenate([kp, kc], axis=0)
            vcat = jnp.concatenate([vp, vc], axis=0)
            s = lax.dot_general(q4.astype(BF16), kcat.astype(BF16), (((1,), (1,)), ((), ())),
                                preferred_element_type=F32) * (HEAD_DIM ** -0.5) - pen
            s = jnp.where(valid, s, NEG_INF)
            w = _sink_softmax(s, sink)
            o = jnp.dot(w.astype(BF16), vcat.astype(BF16), preferred_element_type=F32)
            for g in range(GROUP):
                hcol = (kv * GROUP + g) * HEAD_DIM
                zbuf[rows, hcol:hcol + HEAD_DIM] = o[g * WINDOW:(g + 1) * WINDOW, :]

    u = proj[:, COL_C:COL_C + D_CONV] * proj[:, COL_H:COL_H + D_CONV]
    ubuf[SUBLANES:SUBLANES + tm, :] = u
    off = SUBLANES - (CONV_W - 1)
    y = cw_ref[0:1, :] * ubuf[off:off + tm, :]
    for t in range(1, CONV_W):
        y = y + cw_ref[t:t + 1, :] * ubuf[off + t:off + t + tm, :]
    zbuf[:, D_ATTN:] = proj[:, COL_B:COL_B + D_CONV] * y
    ubuf[0:SUBLANES, :] = ubuf[tm:tm + SUBLANES, :]

    out = jnp.dot(zbuf[...].astype(BF16), wout_ref[...], preferred_element_type=F32)
    x1_ref[...] = x + mod_ref[2:3, :] * out

    kprev[...] = proj[tm - WINDOW:tm, COL_K:COL_K + KV_DIM]
    vprev[...] = proj[tm - WINDOW:tm, COL_V:COL_V + KV_DIM]

    @pl.when(j == pl.num_programs(1) - 1)
    def _():
        ko_ref[...] = proj[tm - WINDOW:tm, COL_K:COL_K + KV_DIM]
        vo_ref[...] = proj[tm - WINDOW:tm, COL_V:COL_V + KV_DIM]
        co_ref[...] = ubuf[SUBLANES - (CONV_W - 1):SUBLANES, :]


def _mix_prompt(x, mod, g_mix, w_in, conv_w, sinks, w_out):
    b, s, _ = x.shape
    tm = MIX_TILE
    const = lambda shape: pl.BlockSpec(shape, lambda i, j: (0,) * len(shape),
                                       pipeline_mode=pl.Buffered(1))
    return pl.pallas_call(
        _mix_prompt_kernel,
        grid=(b, s // tm),
        in_specs=[pl.BlockSpec(memory_space=pltpu.SMEM),
                  pl.BlockSpec((None, tm, D_MODEL), lambda i, j: (i, j, 0)),
                  pl.BlockSpec((None, N_ADA, D_MODEL), lambda i, j: (i, 0, 0)),
                  const((1, D_MODEL)),
                  const((D_MODEL, IN_COLS)),
                  const((CONV_W, D_CONV)),
                  const((D_MODEL, D_MODEL))],
        out_specs=[pl.BlockSpec((None, tm, D_MODEL), lambda i, j: (i, j, 0)),
                   pl.BlockSpec((None, WINDOW, KV_DIM), lambda i, j: (i, 0, 0)),
                   pl.BlockSpec((None, WINDOW, KV_DIM), lambda i, j: (i, 0, 0)),
                   pl.BlockSpec((None, CONV_W - 1, D_CONV), lambda i, j: (i, 0, 0))],
        out_shape=[jax.ShapeDtypeStruct((b, s, D_MODEL), F32),
                   jax.ShapeDtypeStruct((b, WINDOW, KV_DIM), F32),
                   jax.ShapeDtypeStruct((b, WINDOW, KV_DIM), F32),
                   jax.ShapeDtypeStruct((b, CONV_W - 1, D_CONV), F32)],
        scratch_shapes=[pltpu.VMEM((tm, IN_COLS), F32),
                        pltpu.VMEM((tm, D_MODEL), F32),
                        pltpu.VMEM((tm + SUBLANES, D_CONV), F32),
                        pltpu.VMEM((WINDOW, KV_DIM), F32),
                        pltpu.VMEM((WINDOW, KV_DIM), F32)],
        compiler_params=pltpu.CompilerParams(
            dimension_semantics=("arbitrary", "arbitrary"), vmem_limit_bytes=VMEM_LIMIT),
        name="mix_prompt",
    )(sinks, x, mod, g_mix, w_in, conv_w, w_out)


SAMPLE_T = 4
KALL = WINDOW + 2 * SUBLANES


def _mix_sample_kernel(sink_ref, x_ref, mod_ref, g_ref, win_ref, cw_ref, wout_ref,
                       s1_ref, s2_ref, ck_ref, cv_ref,
                       x1_ref, ko_ref, vo_ref, u_ref,
                       proj, zbuf, kall, vall):
    i = pl.program_id(0)
    nseq = ck_ref.shape[0]
    ntok = x_ref.shape[0]

    @pl.when(i == 0)
    def _():
        x = x_ref[...]
        h = _rms(x, g_ref[...]) * (1.0 + mod_ref[1]) + mod_ref[0]
        proj[...] = jnp.dot(h.astype(BF16), win_ref[...], preferred_element_type=F32)
        u = proj[:, COL_C:COL_C + D_CONV] * proj[:, COL_H:COL_H + D_CONV]
        t_of = lax.broadcasted_iota(jnp.int32, (ntok, 1), 0) & (SAMPLE_T - 1)
        u1 = jnp.where(t_of >= 1, pltpu.roll(u, 1, 0), 0.0) + s1_ref[...]
        u2 = jnp.where(t_of >= 2, pltpu.roll(u, 2, 0), 0.0) + s2_ref[...]
        y = cw_ref[0:1, :] * u2 + cw_ref[1:2, :] * u1 + cw_ref[2:3, :] * u
        zbuf[:, D_ATTN:] = proj[:, COL_B:COL_B + D_CONV] * y
        u_ref[...] = u
        for a in (kall, vall):
            a[:, WINDOW + SUBLANES:, :] = jnp.zeros((2, SUBLANES, KV_DIM), F32)

    nrow = GROUP * 2 * SAMPLE_T
    shp = (nrow, KALL)
    row = lax.broadcasted_iota(jnp.int32, shp, 0)
    col = lax.broadcasted_iota(jnp.int32, shp, 1)
    t_row = row & (SAMPLE_T - 1)
    seq_row = (row >> 2) & 1
    new_c = col - WINDOW
    is_cache = col < WINDOW
    dist = jnp.where(is_cache, WINDOW + t_row - col, t_row - (new_c & (SAMPLE_T - 1)))
    valid = ((is_cache & (col > t_row))
             | ((new_c >= 0) & (new_c < 2 * SAMPLE_T) & ((new_c >> 2) == seq_row) & (dist >= 0)))
    distf = dist.astype(F32)
    grp = lax.broadcasted_iota(jnp.int32, (nrow, 1), 0) // (2 * SAMPLE_T)
    pens = [_per_group(grp, SLOPES[kv * GROUP:(kv + 1) * GROUP]) * distf for kv in range(N_KV_HEADS)]
    sinkc = [_per_group(grp, [sink_ref[kv * GROUP + g] for g in range(GROUP)])
             for kv in range(N_KV_HEADS)]
    seq0 = seq_row == 0

    def pair(p, carry):
        r0 = pl.multiple_of(i * (nseq * SAMPLE_T) + p * 2 * SAMPLE_T, 2 * SAMPLE_T)
        rows = pl.ds(r0, 2 * SAMPLE_T)
        knew = proj[rows, COL_K:COL_K + KV_DIM]
        vnew = proj[rows, COL_V:COL_V + KV_DIM]
        for sq in range(2):
            n = 2 * p + sq
            ck = ck_ref[n]
            cv = cv_ref[n]
            kall[sq, 0:WINDOW, :] = ck
            vall[sq, 0:WINDOW, :] = cv
            kall[sq, WINDOW:WINDOW + SUBLANES, :] = knew
            vall[sq, WINDOW:WINDOW + SUBLANES, :] = vnew
            ko_ref[n, 0:WINDOW - SAMPLE_T, :] = ck[SAMPLE_T:, :]
            vo_ref[n, 0:WINDOW - SAMPLE_T, :] = cv[SAMPLE_T:, :]
            ko_ref[n, WINDOW - SAMPLE_T:, :] = knew[sq * SAMPLE_T:(sq + 1) * SAMPLE_T, :]
            vo_ref[n, WINDOW - SAMPLE_T:, :] = vnew[sq * SAMPLE_T:(sq + 1) * SAMPLE_T, :]
        for kv in range(N_KV_HEADS):
            ks = slice(kv * HEAD_DIM, (kv + 1) * HEAD_DIM)
            q = jnp.concatenate(
                [proj[rows, (kv * GROUP + g) * HEAD_DIM:(kv * GROUP + g + 1) * HEAD_DIM]
                 for g in range(GROUP)], axis=0).astype(BF16)
            sc = [lax.dot_general(q, kall[sq, :, ks].astype(BF16), (((1,), (1,)), ((), ())),
                                  preferred_element_type=F32) for sq in range(2)]
            s = jnp.where(seq0, sc[0], sc[1]) * (HEAD_DIM ** -0.5) - pens[kv]
            s = jnp.where(valid, s, NEG_INF)
            w = _sink_softmax(s, sinkc[kv])
            o = (jnp.dot(jnp.where(seq0, w, 0.0).astype(BF16), vall[0, :, ks].astype(BF16),
                         preferred_element_type=F32)
                 + jnp.dot(jnp.where(seq0, 0.0, w).astype(BF16), vall[1, :, ks].astype(BF16),
                           preferred_element_type=F32))
            for g in range(GROUP):
                hcol = (kv * GROUP + g) * HEAD_DIM
                zbuf[rows, hcol:hcol + HEAD_DIM] = o[g * 2 * SAMPLE_T:(g + 1) * 2 * SAMPLE_T, :]
        return carry

    lax.fori_loop(0, nseq // 2, pair, 0)

    @pl.when(i == pl.num_programs(0) - 1)
    def _():
        out = jnp.dot(zbuf[...].astype(BF16), wout_ref[...], preferred_element_type=F32)
        x1_ref[...] = x_ref[...] + mod_ref[2] * out


def _mix_sample(x, mod_tok, g_mix, w_in, conv_w, sinks, w_out, s1, s2, cache_k, cache_v):
    ntok = x.shape[0]
    nseq_all = cache_k.shape[0]
    ns = SAMPLE_SEQS
    const = lambda shape: pl.BlockSpec(shape, lambda i: (0,) * len(shape),
                                       pipeline_mode=pl.Buffered(1))
    return pl.pallas_call(
        _mix_sample_kernel,
        grid=(nseq_all // ns,),
        in_specs=[pl.BlockSpec(memory_space=pltpu.SMEM),
                  const((ntok, D_MODEL)),
                  const((N_ADA // 2, ntok, D_MODEL)),
                  const((1, D_MODEL)),
                  const((D_MODEL, IN_COLS)),
                  const((CONV_W, D_CONV)),
                  const((D_MODEL, D_MODEL)),
                  const((ntok, D_CONV)),
                  const((ntok, D_CONV)),
                  pl.BlockSpec((ns, WINDOW, KV_DIM), lambda i: (i, 0, 0)),
                  pl.BlockSpec((ns, WINDOW, KV_DIM), lambda i: (i, 0, 0))],
        out_specs=[const((ntok, D_MODEL)),
                   pl.BlockSpec((ns, WINDOW, KV_DIM), lambda i: (i, 0, 0)),
                   pl.BlockSpec((ns, WINDOW, KV_DIM), lambda i: (i, 0, 0)),
                   const((ntok, D_CONV))],
        out_shape=[jax.ShapeDtypeStruct((ntok, D_MODEL), F32),
                   jax.ShapeDtypeStruct(cache_k.shape, F32),
                   jax.ShapeDtypeStruct(cache_v.shape, F32),
                   jax.ShapeDtypeStruct((ntok, D_CONV), F32)],
        scratch_shapes=[pltpu.VMEM((ntok, IN_COLS), F32),
                        pltpu.VMEM((ntok, D_MODEL), F32),
                        pltpu.VMEM((2, KALL, KV_DIM), F32),
                        pltpu.VMEM((2, KALL, KV_DIM), F32)],
        compiler_params=pltpu.CompilerParams(
            dimension_semantics=("arbitrary",), vmem_limit_bytes=VMEM_LIMIT),
        name="mix_sample",
    )(sinks, x, mod_tok, g_mix, w_in, conv_w, w_out, s1, s2, cache_k, cache_v)


CAND_COLS = (16, 8, 5, 4, 3, 2, 2, 2)
CAND_ROWS = 16 + SUBLANES * 7 + SUBLANES


def _top16(s):
    n = s.shape[0]
    iota = lax.broadcasted_iota(jnp.int32, s.shape, 0).astype(F32)
    i16 = lax.broadcasted_iota(jnp.int32, (TOPK, s.shape[1]), 0)
    work = s
    rank = jnp.full(s.shape, float(TOPK), F32)
    vals = jnp.zeros((TOPK, s.shape[1]), F32)
    for r in range(TOPK):
        m = jnp.max(work, axis=0, keepdims=True)
        idx = jnp.min(jnp.where(work == m, iota, float(n)), axis=0, keepdims=True)
        sel = iota == idx
        rank = jnp.where(sel, float(r), rank)
        work = jnp.where(sel, -jnp.inf, work)
        vals = jnp.where(i16 == r, m, vals)
    return rank, vals


def _cand_layout(a, b, combine, fill):
    i8 = lax.broadcasted_iota(jnp.int32, (SUBLANES, a.shape[1]), 0)
    pieces = [combine(a[0:1, :], b)]
    for r in range(1, SUBLANES):
        pieces.append(jnp.where(i8 < CAND_COLS[r], combine(a[r:r + 1, :], b[0:SUBLANES, :]), fill))
    pieces.append(combine(a[SUBLANES:, :], b[0:1, :]))
    return jnp.concatenate(pieces, axis=0)


def _select_products(a, b):
    lanes = a.shape[1]
    cand = _cand_layout(a, b, lambda x, y: x + y, -jnp.inf)
    iota = lax.broadcasted_iota(jnp.int32, cand.shape, 0).astype(F32)
    work = cand
    taken = jnp.zeros(cand.shape, F32)
    for _ in range(TOPK):
        m = jnp.max(work, axis=0, keepdims=True)
        idx = jnp.min(jnp.where(work == m, iota, float(CAND_ROWS)), axis=0, keepdims=True)
        sel = iota == idx
        taken = jnp.where(sel, 1.0, taken)
        work = jnp.where(sel, -jnp.inf, work)
    ea = jnp.exp(a - a[0:1, :])
    eb = jnp.exp(b - b[0:1, :])
    z = jnp.sum(taken * _cand_layout(ea, eb, lambda x, y: x * y, 0.0), axis=0, keepdims=True)
    i8 = lax.broadcasted_iota(jnp.int32, (SUBLANES, lanes), 0)
    low = jnp.zeros((SUBLANES, lanes), F32)
    low = jnp.where(i8 == 0, jnp.sum(taken[0:TOPK, :], axis=0, keepdims=True), low)
    for r in range(1, SUBLANES):
        base = TOPK + SUBLANES * (r - 1)
        low = jnp.where(i8 == r, jnp.sum(taken[base:base + SUBLANES, :], axis=0, keepdims=True), low)
    ncol = jnp.concatenate([low, taken[CAND_ROWS - SUBLANES:, :]], axis=0)
    return ncol, z


def _peer_kernel(x1_ref, sh_ref, sc_ref, gt_ref, gffn_ref, gfin_ref, wqt_ref, keys_ref, u_ref, v_ref,
                 y_ref,
                 h2t, qt, sbuf, rk1, e1, nsel, e0n, act, wbuf, acc):
    eb = pl.program_id(1)
    tt = x1_ref.shape[0]
    nchunk = tt // LANES
    eblk = u_ref.shape[0]

    @pl.when(eb == 0)
    def _():
        h2 = _rms(x1_ref[...], gffn_ref[...]) * (1.0 + sc_ref[...]) + sh_ref[...]
        h2t[...] = h2.T.astype(BF16)
        qt[...] = jnp.dot(wqt_ref[...], h2t[...], preferred_element_type=F32).astype(BF16)
        acc[...] = jnp.zeros_like(acc)

        def head(h, carry):
            for p in range(2):
                r0 = pl.multiple_of((2 * h + p) * N_KEYS, N_KEYS)
                s = jnp.dot(keys_ref[2 * h + p], qt[pl.ds(r0, N_KEYS), :],
                            preferred_element_type=F32)
                for c in range(nchunk):
                    sbuf[p, c] = s[:, c * LANES:(c + 1) * LANES]

            def chunk(c, carry2):
                s0 = sbuf[0, c]
                s1 = sbuf[1, c]
                rank0, a = _top16(s0)
                rank1, b = _top16(s1)
                ncol, z = _select_products(a, b)
                ns = jnp.zeros(s0.shape, F32)
                for r in range(TOPK):
                    ns = jnp.where(rank0 == float(r), ncol[r:r + 1, :], ns)
                nsel[c, h] = ns
                rk1[c, h] = rank1
                e0n[c, h] = jnp.exp(s0 - a[0:1, :]) / z
                e1[c, h] = jnp.exp(s1 - b[0:1, :])
                return carry2

            lax.fori_loop(0, nchunk, chunk, 0)
            return carry

        lax.fori_loop(0, PEER_HEADS, head, 0)

    act[...] = jnp.dot(u_ref[...], h2t[...], preferred_element_type=F32)

    def rowgroup(il, carry):
        i_glob = eb * (eblk // N_KEYS) + il
        rows = pl.ds(pl.multiple_of(il * N_KEYS, N_KEYS), N_KEYS)
        for c in range(nchunk):
            lanes = slice(c * LANES, (c + 1) * LANES)
            g = jnp.zeros((N_KEYS, LANES), F32)
            for h in range(PEER_HEADS):
                ns = nsel[c, h, pl.ds(i_glob, 1), :]
                e0 = e0n[c, h, pl.ds(i_glob, 1), :]
                g = g + jnp.where(rk1[c, h] < ns, e1[c, h] * e0, 0.0)
            a = act[rows, lanes]
            gelu = 0.5 * a * (1.0 + lax.erf(a * SQRT_HALF))
            wbuf[rows, lanes] = (g * gelu).astype(BF16)
        return carry

    lax.fori_loop(0, eblk // N_KEYS, rowgroup, 0)
    acc[...] += lax.dot_general(wbuf[...], v_ref[...], (((0,), (0,)), ((), ())),
                                preferred_element_type=F32)

    @pl.when(eb == pl.num_programs(1) - 1)
    def _():
        x2 = x1_ref[...] + gt_ref[...] * acc[...]
        y_ref[...] = _rms(x2, gfin_ref[...])


def _peer(x1, sh2, sc2, g2, g_ffn, g_fin, wq_t, keys, u_tab, v_tab, rows_per_mod):
    t = x1.shape[0]
    tt = PEER_TILE
    eblk = PEER_EBLK
    nchunk = tt // LANES
    if rows_per_mod is None:
        mod_spec = pl.BlockSpec((None, tt, D_MODEL), lambda i, e: (0, i, 0))
    else:
        per = rows_per_mod // tt
        mod_spec = pl.BlockSpec((None, 1, D_MODEL), lambda i, e: (i // per, 0, 0))
    const = lambda shape: pl.BlockSpec(shape, lambda i, e: (0,) * len(shape),
                                       pipeline_mode=pl.Buffered(1))
    stat = pltpu.VMEM((nchunk, PEER_HEADS, N_KEYS, LANES), F32)
    return pl.pallas_call(
        _peer_kernel,
        grid=(t // tt, N_EXPERTS // eblk),
        in_specs=[pl.BlockSpec((tt, D_MODEL), lambda i, e: (i, 0)),
                  mod_spec, mod_spec, mod_spec,
                  const((1, D_MODEL)), const((1, D_MODEL)),
                  const((2 * PEER_HEADS * N_KEYS, D_MODEL)),
                  const((2 * PEER_HEADS, N_KEYS, N_KEYS)),
                  pl.BlockSpec((eblk, D_MODEL), lambda i, e: (e, 0)),
                  pl.BlockSpec((eblk, D_MODEL), lambda i, e: (e, 0))],
        out_specs=pl.BlockSpec((tt, D_MODEL), lambda i, e: (i, 0)),
        out_shape=jax.ShapeDtypeStruct((t, D_MODEL), F32),
        scratch_shapes=[pltpu.VMEM((D_MODEL, tt), BF16),
                        pltpu.VMEM((2 * PEER_HEADS * N_KEYS, tt), BF16),
                        pltpu.VMEM((2, nchunk, N_KEYS, LANES), F32),
                        stat, stat, stat, stat,
                        pltpu.VMEM((eblk, tt), F32),
                        pltpu.VMEM((eblk, tt), BF16),
                        pltpu.VMEM((tt, D_MODEL), F32)],
        compiler_params=pltpu.CompilerParams(
            dimension_semantics=("arbitrary", "arbitrary"), vmem_limit_bytes=VMEM_LIMIT),
        name="peer",
    )(x1, sh2, sc2, g2, g_ffn, g_fin, wq_t, keys, u_tab, v_tab)


def kernel(x_prompt, x_sample, c_prompt, c_sample, cache_k, cache_v, state_conv, norm_mix_g, norm_ffn_g,
           w_ada, b_ada, w_in, conv_w, attn_sinks, w_out, peer_wq, peer_keys, peer_u, peer_v,
           norm_final_g):
    depth = w_ada.shape[0]
    assert depth == 1
    nb, seq, _ = x_prompt.shape
    ns, nt, _ = x_sample.shape
    assert nt == SAMPLE_T
    l = 0

    mod = _ada(jnp.concatenate([c_prompt, c_sample], axis=0), w_ada[l], b_ada[l])
    mod_p = jnp.transpose(mod[:, :nb], (1, 0, 2))
    mod_s = jnp.repeat(mod[:, nb:], nt, axis=1)

    g_mix = norm_mix_g[l].reshape(1, -1)
    g_ffn = norm_ffn_g[l].reshape(1, -1)
    g_fin = norm_final_g.reshape(1, -1)
    w_in_b = w_in[l].astype(BF16)
    w_out_b = w_out[l].astype(BF16)
    wq_t = peer_wq[l].T.astype(BF16)
    keys = peer_keys[l].reshape(2 * PEER_HEADS, N_KEYS, N_KEYS).astype(BF16)
    u_tab = peer_u[l].astype(BF16)
    v_tab = peer_v[l].astype(BF16)

    x1_p, k_p, v_p, conv_p = _mix_prompt(x_prompt, mod_p, g_mix, w_in_b, conv_w[l], attn_sinks[l], w_out_b)
    y_p = _peer(x1_p.reshape(nb * seq, D_MODEL), mod_p[:, 3:4], mod_p[:, 4:5], mod_p[:, 5:6],
                g_ffn, g_fin, wq_t, keys, u_tab, v_tab, rows_per_mod=seq)

    st = state_conv[l]
    zero = jnp.zeros_like(st[:, 0])
    s1 = jnp.stack([st[:, 1], zero, zero, zero], axis=1).reshape(ns * nt, D_CONV)
    s2 = jnp.stack([st[:, 0], st[:, 1], zero, zero], axis=1).reshape(ns * nt, D_CONV)
    x1_s, k_s, v_s, u_s = _mix_sample(
        x_sample.reshape(ns * nt, D_MODEL), mod_s[:N_ADA // 2], g_mix, w_in_b, conv_w[l], attn_sinks[l], w_out_b,
        s1, s2, cache_k[l].reshape(ns, WINDOW, KV_DIM), cache_v[l].reshape(ns, WINDOW, KV_DIM))
    y_s = _peer(x1_s, mod_s[3:4], mod_s[4:5], mod_s[5:6], g_ffn, g_fin, wq_t, keys, u_tab, v_tab,
                rows_per_mod=None)
    conv_s = u_s.reshape(ns, nt, D_CONV)[:, nt - (CONV_W - 1):]

    kv_shape = (1, -1, WINDOW, N_KV_HEADS, HEAD_DIM)
    return (y_p.reshape(nb, seq, D_MODEL), y_s.reshape(ns, nt, D_MODEL),
            k_p.reshape(kv_shape), v_p.reshape(kv_shape), conv_p[None],
            k_s.reshape(kv_shape), v_s.reshape(kv_shape), conv_s[None])
```

```python
import functools

import jax
import jax.numpy as jnp
import numpy as np
from jax import lax
from jax.experimental import pallas as pl
from jax.experimental.pallas import tpu as pltpu

F32 = jnp.float32
BF16 = jnp.bfloat16

D_MODEL = 1024
D_ATTN = 512
D_CONV = 512
HEAD_DIM = 64
N_HEADS = 8
N_KV_HEADS = 2
GROUP = 4
KV_DIM = 128
WINDOW = 128
CONV_W = 3
IN_COLS = D_ATTN + 2 * KV_DIM + 3 * D_CONV
COL_K = D_ATTN
COL_V = D_ATTN + KV_DIM
COL_B = D_ATTN + 2 * KV_DIM
COL_C = COL_B + D_CONV
COL_H = COL_C + D_CONV
N_ADA = 6
PEER_HEADS = 8
N_KEYS = 128
N_EXPERTS = N_KEYS * N_KEYS
TOPK = 16
EPS = 1e-6
NEG_INF = -1e30
SQRT_HALF = float(np.sqrt(0.5))

LANES = 128
SUBLANES = 8
MIX_TILE = 512
SAMPLE_SEQS = 16
PEER_TILE = 512
PEER_EBLK = 1024
BF16_ROWS = 16
VMEM_LIMIT = 56 * 1024 * 1024

SLOPES = tuple(float(2.0 ** (-8.0 * (h + 1) / N_HEADS)) for h in range(N_HEADS))


def _rms(x, g):
    return x * lax.rsqrt(jnp.mean(x * x, axis=-1, keepdims=True) + EPS) * g


def _per_group(rows_group, vals):
    out = jnp.full(rows_group.shape, vals[0], F32)
    for g in range(1, len(vals)):
        out = jnp.where(rows_group == g, vals[g], out)
    return out


def _ada_kernel(c_ref, w_ref, b_ref, o_ref):
    s = jax.nn.silu(c_ref[...])
    o_ref[...] = jnp.dot(s.astype(BF16), w_ref[...].astype(BF16),
                         preferred_element_type=F32) + b_ref[...]


def _ada(c, w_ada, b_ada):
    n = c.shape[0]
    return pl.pallas_call(
        _ada_kernel,
        grid=(N_ADA,),
        in_specs=[pl.BlockSpec((n, D_MODEL), lambda j: (0, 0)),
                  pl.BlockSpec((D_MODEL, D_MODEL), lambda j: (0, j)),
                  pl.BlockSpec((1, D_MODEL), lambda j: (0, j))],
        out_specs=pl.BlockSpec((None, n, D_MODEL), lambda j: (j, 0, 0)),
        out_shape=jax.ShapeDtypeStruct((N_ADA, n, D_MODEL), F32),
        name="ada",
    )(c, w_ada, b_ada.reshape(1, -1))


def _sink_softmax(s, sink):
    m = jnp.maximum(jnp.max(s, axis=-1, keepdims=True), sink)
    p = jnp.exp(s - m)
    den = jnp.sum(p, axis=-1, keepdims=True) + jnp.exp(sink - m)
    return p / den


def _mix_prompt_kernel(sink_ref, x_ref, mod_ref, g_ref, win_ref, cw_ref, wout_ref,
                       x1_ref, ko_ref, vo_ref, co_ref,
                       proj, zbuf, ubuf, kprev, vprev):
    j = pl.program_id(1)
    tm = x_ref.shape[0]
    nblk = tm // WINDOW

    @pl.when(j == 0)
    def _():
        kprev[...] = jnp.zeros_like(kprev)
        vprev[...] = jnp.zeros_like(vprev)
        ubuf[0:SUBLANES, :] = jnp.zeros((SUBLANES, D_CONV), F32)

    x = x_ref[...]
    h = _rms(x, g_ref[...]) * (1.0 + mod_ref[1:2, :]) + mod_ref[0:1, :]
    proj[...] = jnp.dot(h.astype(BF16), win_ref[...], preferred_element_type=F32)

    shp = (GROUP * WINDOW, 2 * WINDOW)
    row = lax.broadcasted_iota(jnp.int32, shp, 0)
    col = lax.broadcasted_iota(jnp.int32, shp, 1)
    dist = (row & (WINDOW - 1)) + WINDOW - col
    valid_any = (dist >= 0) & (dist < WINDOW)
    distf = dist.astype(F32)
    grp = lax.broadcasted_iota(jnp.int32, (GROUP * WINDOW, 1), 0) // WINDOW
    first_col = jnp.where(j == 0, WINDOW, 0)

    for kv in range(N_KV_HEADS):
        pen = _per_group(grp, SLOPES[kv * GROUP:(kv + 1) * GROUP]) * distf
        sink = _per_group(grp, [sink_ref[kv * GROUP + g] for g in range(GROUP)])
        ks = slice(kv * HEAD_DIM, (kv + 1) * HEAD_DIM)
        for b in range(nblk):
            rows = slice(b * WINDOW, (b + 1) * WINDOW)
            if b == 0:
                kp, vp = kprev[:, ks], vprev[:, ks]
                valid = valid_any & (col >= first_col)
            else:
                prows = slice((b - 1) * WINDOW, b * WINDOW)
                kp = proj[prows, COL_K + kv * HEAD_DIM:COL_K + (kv + 1) * HEAD_DIM]
                vp = proj[prows, COL_V + kv * HEAD_DIM:COL_V + (kv + 1) * HEAD_DIM]
                valid = valid_any
            kc = proj[rows, COL_K + kv * HEAD_DIM:COL_K + (kv + 1) * HEAD_DIM]
            vc = proj[rows, COL_V + kv * HEAD_DIM:COL_V + (kv + 1) * HEAD_DIM]
            q4 = jnp.concatenate(
                [proj[rows, (kv * GROUP + g) * HEAD_DIM:(kv * GROUP + g + 1) * HEAD_DIM]
                 for g in range(GROUP)], axis=0)
            kcat = jnp.concatenate([kp, kc], axis=0)
            vcat = jnp.concatenate([vp, vc], axis=0)
            s = lax.dot_general(q4.astype(BF16), kcat.astype(BF16), (((1,), (1,)), ((), ())),
                                preferred_element_type=F32) * (HEAD_DIM ** -0.5) - pen
            s = jnp.where(valid, s, NEG_INF)
            w = _sink_softmax(s, sink)
            o = jnp.dot(w.astype(BF16), vcat.astype(BF16), preferred_element_type=F32)
            for g in range(GROUP):
                hcol = (kv * GROUP + g) * HEAD_DIM
                zbuf[rows, hcol:hcol + HEAD_DIM] = o[g * WINDOW:(g + 1) * WINDOW, :]

    u = proj[:, COL_C:COL_C + D_CONV] * proj[:, COL_H:COL_H + D_CONV]
    ubuf[SUBLANES:SUBLANES + tm, :] = u
    off = SUBLANES - (CONV_W - 1)
    y = cw_ref[0:1, :] * ubuf[off:off + tm, :]
    for t in range(1, CONV_W):
        y = y + cw_ref[t:t + 1, :] * ubuf[off + t:off + t + tm, :]
    zbuf[:, D_ATTN:] = proj[:, COL_B:COL_B + D_CONV] * y
    ubuf[0:SUBLANES, :] = ubuf[tm:tm + SUBLANES, :]

    out = jnp.dot(zbuf[...].astype(BF16), wout_ref[...], preferred_element_type=F32)
    x1_ref[...] = x + mod_ref[2:3, :] * out

    kprev[...] = proj[tm - WINDOW:tm, COL_K:COL_K + KV_DIM]
    vprev[...] = proj[tm - WINDOW:tm, COL_V:COL_V + KV_DIM]

    @pl.when(j == pl.num_programs(1) - 1)
    def _():
        ko_ref[...] = proj[tm - WINDOW:tm, COL_K:COL_K + KV_DIM]
        vo_ref[...] = proj[tm - WINDOW:tm, COL_V:COL_V + KV_DIM]
        co_ref[...] = ubuf[SUBLANES - (CONV_W - 1):SUBLANES, :]


def _mix_prompt(x, mod, g_mix, w_in, conv_w, sinks, w_out):
    b, s, _ = x.shape
    tm = MIX_TILE
    const = lambda shape: pl.BlockSpec(shape, lambda i, j: (0,) * len(shape),
                                       pipeline_mode=pl.Buffered(1))
    return pl.pallas_call(
        _mix_prompt_kernel,
        grid=(b, s // tm),
        in_specs=[pl.BlockSpec(memory_space=pltpu.SMEM),
                  pl.BlockSpec((None, tm, D_MODEL), lambda i, j: (i, j, 0)),
                  pl.BlockSpec((None, N_ADA, D_MODEL), lambda i, j: (i, 0, 0)),
                  const((1, D_MODEL)),
                  const((D_MODEL, IN_COLS)),
                  const((CONV_W, D_CONV)),
                  const((D_MODEL, D_MODEL))],
        out_specs=[pl.BlockSpec((None, tm, D_MODEL), lambda i, j: (i, j, 0)),
                   pl.BlockSpec((None, WINDOW, KV_DIM), lambda i, j: (i, 0, 0)),
                   pl.BlockSpec((None, WINDOW, KV_DIM), lambda i, j: (i, 0, 0)),
                   pl.BlockSpec((None, CONV_W - 1, D_CONV), lambda i, j: (i, 0, 0))],
        out_shape=[jax.ShapeDtypeStruct((b, s, D_MODEL), F32),
                   jax.ShapeDtypeStruct((b, WINDOW, KV_DIM), F32),
                   jax.ShapeDtypeStruct((b, WINDOW, KV_DIM), F32),
                   jax.ShapeDtypeStruct((b, CONV_W - 1, D_CONV), F32)],
        scratch_shapes=[pltpu.VMEM((tm, IN_COLS), F32),
                        pltpu.VMEM((tm, D_MODEL), F32),
                        pltpu.VMEM((tm + SUBLANES, D_CONV), F32),
                        pltpu.VMEM((WINDOW, KV_DIM), F32),
                        pltpu.VMEM((WINDOW, KV_DIM), F32)],
        compiler_params=pltpu.CompilerParams(
            dimension_semantics=("arbitrary", "arbitrary"), vmem_limit_bytes=VMEM_LIMIT),
        name="mix_prompt",
    )(sinks, x, mod, g_mix, w_in, conv_w, w_out)


SAMPLE_T = 4
KALL = WINDOW + 2 * SUBLANES


def _mix_sample_kernel(sink_ref, x_ref, mod_ref, g_ref, win_ref, cw_ref, wout_ref,
                       s1_ref, s2_ref, ck_ref, cv_ref,
                       x1_ref, ko_ref, vo_ref, u_ref,
                       proj, zbuf, kall, vall):
    i = pl.program_id(0)
    nseq = ck_ref.shape[0]
    ntok = x_ref.shape[0]

    @pl.when(i == 0)
    def _():
        x = x_ref[...]
        h = _rms(x, g_ref[...]) * (1.0 + mod_ref[1]) + mod_ref[0]
        proj[...] = jnp.dot(h.astype(BF16), win_ref[...], preferred_element_type=F32)
        u = proj[:, COL_C:COL_C + D_CONV] * proj[:, COL_H:COL_H + D_CONV]
        t_of = lax.broadcasted_iota(jnp.int32, (ntok, 1), 0) & (SAMPLE_T - 1)
        u1 = jnp.where(t_of >= 1, pltpu.roll(u, 1, 0), 0.0) + s1_ref[...]
        u2 = jnp.where(t_of >= 2, pltpu.roll(u, 2, 0), 0.0) + s2_ref[...]
        y = cw_ref[0:1, :] * u2 + cw_ref[1:2, :] * u1 + cw_ref[2:3, :] * u
        zbuf[:, D_ATTN:] = proj[:, COL_B:COL_B + D_CONV] * y
        u_ref[...] = u
        for a in (kall, vall):
            a[:, WINDOW + SUBLANES:, :] = jnp.zeros((2, SUBLANES, KV_DIM), F32)

    nrow = GROUP * 2 * SAMPLE_T
    shp = (nrow, KALL)
    row = lax.broadcasted_iota(jnp.int32, shp, 0)
    col = lax.broadcasted_iota(jnp.int32, shp, 1)
    t_row = row & (SAMPLE_T - 1)
    seq_row = (row >> 2) & 1
    new_c = col - WINDOW
    is_cache = col < WINDOW
    dist = jnp.where(is_cache, WINDOW + t_row - col, t_row - (new_c & (SAMPLE_T - 1)))
    valid = ((is_cache & (col > t_row))
             | ((new_c >= 0) & (new_c < 2 * SAMPLE_T) & ((new_c >> 2) == seq_row) & (dist >= 0)))
    distf = dist.astype(F32)
    grp = lax.broadcasted_iota(jnp.int32, (nrow, 1), 0) // (2 * SAMPLE_T)
    pens = [_per_group(grp, SLOPES[kv * GROUP:(kv + 1) * GROUP]) * distf for kv in range(N_KV_HEADS)]
    sinkc = [_per_group(grp, [sink_ref[kv * GROUP + g] for g in range(GROUP)])
             for kv in range(N_KV_HEADS)]
    seq0 = seq_row == 0

    def pair(p, carry):
        r0 = pl.multiple_of(i * (nseq * SAMPLE_T) + p * 2 * SAMPLE_T, 2 * SAMPLE_T)
        rows = pl.ds(r0, 2 * SAMPLE_T)
        knew = proj[rows, COL_K:COL_K + KV_DIM]
        vnew = proj[rows, COL_V:COL_V + KV_DIM]
        for sq in range(2):
            n = 2 * p + sq
            ck = ck_ref[n]
            cv = cv_ref[n]
            kall[sq, 0:WINDOW, :] = ck
            vall[sq, 0:WINDOW, :] = cv
            kall[sq, WINDOW:WINDOW + SUBLANES, :] = knew
            vall[sq, WINDOW:WINDOW + SUBLANES, :] = vnew
            ko_ref[n, 0:WINDOW - SAMPLE_T, :] = ck[SAMPLE_T:, :]
            vo_ref[n, 0:WINDOW - SAMPLE_T, :] = cv[SAMPLE_T:, :]
            ko_ref[n, WINDOW - SAMPLE_T:, :] = knew[sq * SAMPLE_T:(sq + 1) * SAMPLE_T, :]
            vo_ref[n, WINDOW - SAMPLE_T:, :] = vnew[sq * SAMPLE_T:(sq + 1) * SAMPLE_T, :]
        for kv in range(N_KV_HEADS):
            ks = slice(kv * HEAD_DIM, (kv + 1) * HEAD_DIM)
            q = jnp.concatenate(
                [proj[rows, (kv * GROUP + g) * HEAD_DIM:(kv * GROUP + g + 1) * HEAD_DIM]
                 for g in range(GROUP)], axis=0).astype(BF16)
            sc = [lax.dot_general(q, kall[sq, :, ks].astype(BF16), (((1,), (1,)), ((), ())),
                                  preferred_element_type=F32) for sq in range(2)]
            s = jnp.where(seq0, sc[0], sc[1]) * (HEAD_DIM ** -0.5) - pens[kv]
            s = jnp.where(valid, s, NEG_INF)
            w = _sink_softmax(s, sinkc[kv])
            o = (jnp.dot(jnp.where(seq0, w, 0.0).astype(BF16), vall[0, :, ks].astype(BF16),
                         preferred_element_type=F32)
                 + jnp.dot(jnp.where(seq0, 0.0, w).astype(BF16), vall[1, :, ks].astype(BF16),
                           preferred_element_type=F32))
            for g in range(GROUP):
                hcol = (kv * GROUP + g) * HEAD_DIM
                zbuf[rows, hcol:hcol + HEAD_DIM] = o[g * 2 * SAMPLE_T:(g + 1) * 2 * SAMPLE_T, :]
        return carry

    lax.fori_loop(0, nseq // 2, pair, 0)

    @pl.when(i == pl.num_programs(0) - 1)
    def _():
        out = jnp.dot(zbuf[...].astype(BF16), wout_ref[...], preferred_element_type=F32)
        x1_ref[...] = x_ref[...] + mod_ref[2] * out


def _mix_sample(x, mod_tok, g_mix, w_in, conv_w, sinks, w_out, s1, s2, cache_k, cache_v):
    ntok = x.shape[0]
    nseq_all = cache_k.shape[0]
    ns = SAMPLE_SEQS
    const = lambda shape: pl.BlockSpec(shape, lambda i: (0,) * len(shape),
                                       pipeline_mode=pl.Buffered(1))
    return pl.pallas_call(
        _mix_sample_kernel,
        grid=(nseq_all // ns,),
        in_specs=[pl.BlockSpec(memory_space=pltpu.SMEM),
                  const((ntok, D_MODEL)),
                  const((N_ADA // 2, ntok, D_MODEL)),
                  const((1, D_MODEL)),
                  const((D_MODEL, IN_COLS)),
                  const((CONV_W, D_CONV)),
                  const((D_MODEL, D_MODEL)),
                  const((ntok, D_CONV)),
                  const((ntok, D_CONV)),
                  pl.BlockSpec((ns, WINDOW, KV_DIM), lambda i: (i, 0, 0)),
                  pl.BlockSpec((ns, WINDOW, KV_DIM), lambda i: (i, 0, 0))],
        out_specs=[const((ntok, D_MODEL)),
                   pl.BlockSpec((ns, WINDOW, KV_DIM), lambda i: (i, 0, 0)),
                   pl.BlockSpec((ns, WINDOW, KV_DIM), lambda i: (i, 0, 0)),
                   const((ntok, D_CONV))],
        out_shape=[jax.ShapeDtypeStruct((ntok, D_MODEL), F32),
                   jax.ShapeDtypeStruct(cache_k.shape, F32),
                   jax.ShapeDtypeStruct(cache_v.shape, F32),
                   jax.ShapeDtypeStruct((ntok, D_CONV), F32)],
        scratch_shapes=[pltpu.VMEM((ntok, IN_COLS), F32),
                        pltpu.VMEM((ntok, D_MODEL), F32),
                        pltpu.VMEM((2, KALL, KV_DIM), F32),
                        pltpu.VMEM((2, KALL, KV_DIM), F32)],
        compiler_params=pltpu.CompilerParams(
            dimension_semantics=("arbitrary",), vmem_limit_bytes=VMEM_LIMIT),
        name="mix_sample",
    )(sinks, x, mod_tok, g_mix, w_in, conv_w, w_out, s1, s2, cache_k, cache_v)


CAND_COLS = (16, 8, 5, 4, 3, 2, 2, 2)
CAND_ROWS = 16 + SUBLANES * 7 + SUBLANES


def _top16(s):
    n = s.shape[0]
    iota = lax.broadcasted_iota(jnp.int32, s.shape, 0).astype(F32)
    i16 = lax.broadcasted_iota(jnp.int32, (TOPK, s.shape[1]), 0)
    work = s
    rank = jnp.full(s.shape, float(TOPK), F32)
    vals = jnp.zeros((TOPK, s.shape[1]), F32)
    for r in range(TOPK):
        m = jnp.max(work, axis=0, keepdims=True)
        idx = jnp.min(jnp.where(work == m, iota, float(n)), axis=0, keepdims=True)
        sel = iota == idx
        rank = jnp.where(sel, float(r), rank)
        work = jnp.where(sel, -jnp.inf, work)
        vals = jnp.where(i16 == r, m, vals)
    return rank, vals


def _cand_layout(a, b, combine, fill):
    i8 = lax.broadcasted_iota(jnp.int32, (SUBLANES, a.shape[1]), 0)
    pieces = [combine(a[0:1, :], b)]
    for r in range(1, SUBLANES):
        pieces.append(jnp.where(i8 < CAND_COLS[r], combine(a[r:r + 1, :], b[0:SUBLANES, :]), fill))
    pieces.append(combine(a[SUBLANES:, :], b[0:1, :]))
    return jnp.concatenate(pieces, axis=0)


def _select_products(a, b):
    lanes = a.shape[1]
    cand = _cand_layout(a, b, lambda x, y: x + y, -jnp.inf)
    iota = lax.broadcasted_iota(jnp.int32, cand.shape, 0).astype(F32)
    work = cand
    taken = jnp.zeros(cand.shape, F32)
    for _ in range(TOPK):
        m = jnp.max(work, axis=0, keepdims=True)
        idx = jnp.min(jnp.where(work == m, iota, float(CAND_ROWS)), axis=0, keepdims=True)
        sel = iota == idx
        taken = jnp.where(sel, 1.0, taken)
        work = jnp.where(sel, -jnp.inf, work)
    ea = jnp.exp(a - a[0:1, :])
    eb = jnp.exp(b - b[0:1, :])
    z = jnp.sum(taken * _cand_layout(ea, eb, lambda x, y: x * y, 0.0), axis=0, keepdims=True)
    i8 = lax.broadcasted_iota(jnp.int32, (SUBLANES, lanes), 0)
    low = jnp.zeros((SUBLANES, lanes), F32)
    low = jnp.where(i8 == 0, jnp.sum(taken[0:TOPK, :], axis=0, keepdims=True), low)
    for r in range(1, SUBLANES):
        base = TOPK + SUBLANES * (r - 1)
        low = jnp.where(i8 == r, jnp.sum(taken[base:base + SUBLANES, :], axis=0, keepdims=True), low)
    ncol = jnp.concatenate([low, taken[CAND_ROWS - SUBLANES:, :]], axis=0)
    return ncol, z


def _peer_kernel(xsel_ref, xfin_ref, sh_ref, sc_ref, gt_ref, gffn_ref, gfin_ref, wqt_ref, keys_ref,
                 u_ref, v_ref,
                 y_ref,
                 h2t, qt, sbuf, rk1, e1, nsel, e0n, acc, act0, act1, w0, w1, *, ntiles):
    s = pl.program_id(0)
    tt = xsel_ref.shape[0]
    nchunk = tt // LANES
    eblk = u_ref.shape[0]
    nblk = N_EXPERTS // eblk
    nsteps = ntiles * nblk
    act, wbuf = (act0, act1), (w0, w1)
    new_tile = (s % nblk == 0) & (s < nsteps)

    @pl.when(s == 0)
    def _():
        for ref in (rk1, e1, nsel, e0n, acc, act0, act1, w0, w1):
            ref[...] = jnp.zeros(ref.shape, ref.dtype)

    @pl.when(new_tile)
    def _():
        h2 = _rms(xsel_ref[...], gffn_ref[...]) * (1.0 + sc_ref[...]) + sh_ref[...]
        h2t[...] = h2.T.astype(BF16)

    def retrieval():
        qt[...] = jnp.dot(wqt_ref[...], h2t[...], preferred_element_type=F32).astype(BF16)

        def head(h, carry):
            for p in range(2):
                r0 = pl.multiple_of((2 * h + p) * N_KEYS, N_KEYS)
                s = jnp.dot(keys_ref[2 * h + p], qt[pl.ds(r0, N_KEYS), :],
                            preferred_element_type=F32)
                for c in range(nchunk):
                    sbuf[p, c] = s[:, c * LANES:(c + 1) * LANES]

            def chunk(c, carry2):
                s0 = sbuf[0, c]
                s1 = sbuf[1, c]
                rank0, a = _top16(s0)
                rank1, b = _top16(s1)
                ncol, z = _select_products(a, b)
                ns = jnp.zeros(s0.shape, F32)
                for r in range(TOPK):
                    ns = jnp.where(rank0 == float(r), ncol[r:r + 1, :], ns)
                nsel[c, h] = ns
                rk1[c, h] = rank1.astype(BF16)
                e0n[c, h] = jnp.exp(s0 - a[0:1, :]) / z
                e1[c, h] = jnp.exp(s1 - b[0:1, :]).astype(BF16)
                return carry2

            lax.fori_loop(0, nchunk, chunk, 0)
            return carry

        lax.fori_loop(0, PEER_HEADS, head, 0)

    i_base = (jnp.maximum(s - 1, 0) % nblk) * (eblk // N_KEYS)

    def stages(p):
        act[p][...] = jnp.dot(u_ref[...], h2t[...], preferred_element_type=F32)

        for ig in range(eblk // N_KEYS):
            i_glob = i_base + ig
            for c in range(nchunk):
                lanes = slice(c * LANES, (c + 1) * LANES)
                g = [None] * (N_KEYS // BF16_ROWS)
                for h in range(PEER_HEADS):
                    ns = jnp.broadcast_to(nsel[c, h, pl.ds(i_glob, 1), :],
                                          (BF16_ROWS, LANES)).astype(BF16)
                    e0 = jnp.broadcast_to(e0n[c, h, pl.ds(i_glob, 1), :],
                                          (BF16_ROWS, LANES)).astype(BF16)
                    for jg in range(N_KEYS // BF16_ROWS):
                        r = slice(jg * BF16_ROWS, (jg + 1) * BF16_ROWS)
                        t = jnp.where(rk1[c, h, r, :] < ns, e1[c, h, r, :] * e0, 0)
                        g[jg] = t if h == 0 else g[jg] + t
                for jg in range(N_KEYS // BF16_ROWS):
                    rows = slice(ig * N_KEYS + jg * BF16_ROWS, ig * N_KEYS + (jg + 1) * BF16_ROWS)
                    a = act[1 - p][rows, lanes].astype(BF16)
                    gelu = 0.5 * a * (1 + lax.erf(a * SQRT_HALF))
                    wbuf[p][rows, lanes] = g[jg] * gelu

        acc[...] += lax.dot_general(wbuf[1 - p][...], v_ref[...], (((0,), (0,)), ((), ())),
                                    preferred_element_type=F32)

    for p in range(2):
        pl.when(s % 2 == p)(functools.partial(stages, p))

    pl.when(new_tile)(retrieval)

    @pl.when((s >= 2) & ((s - 2) % nblk == nblk - 1))
    def _():
        x2 = xfin_ref[...] + gt_ref[...] * acc[...]
        y_ref[...] = _rms(x2, gfin_ref[...])
        acc[...] = jnp.zeros_like(acc)


def _peer(x1, sh2, sc2, g2, g_ffn, g_fin, wq_t, keys, u_tab, v_tab, rows_per_mod):
    t = x1.shape[0]
    tt = PEER_TILE
    eblk = PEER_EBLK
    nchunk = tt // LANES
    ntiles = t // tt
    nblk = N_EXPERTS // eblk
    nsteps = ntiles * nblk
    sel_tile = lambda s: jnp.minimum(s // nblk, ntiles - 1)
    fin_tile = lambda s: jnp.maximum(s - 2, 0) // nblk
    once = pl.Buffered(1)
    if rows_per_mod is None:
        mod_spec = lambda tile: pl.BlockSpec((None, tt, D_MODEL), lambda s: (0, tile(s), 0),
                                             pipeline_mode=once)
    else:
        per = rows_per_mod // tt
        mod_spec = lambda tile: pl.BlockSpec((None, 1, D_MODEL), lambda s: (tile(s) // per, 0, 0),
                                             pipeline_mode=once)
    const = lambda shape: pl.BlockSpec(shape, lambda s: (0,) * len(shape), pipeline_mode=once)
    stat = lambda dt: pltpu.VMEM((nchunk, PEER_HEADS, N_KEYS, LANES), dt)
    return pl.pallas_call(
        functools.partial(_peer_kernel, ntiles=ntiles),
        grid=(nsteps + 2,),
        in_specs=[pl.BlockSpec((tt, D_MODEL), lambda s: (sel_tile(s), 0), pipeline_mode=once),
                  pl.BlockSpec((tt, D_MODEL), lambda s: (fin_tile(s), 0), pipeline_mode=once),
                  mod_spec(sel_tile), mod_spec(sel_tile), mod_spec(fin_tile),
                  const((1, D_MODEL)), const((1, D_MODEL)),
                  const((2 * PEER_HEADS * N_KEYS, D_MODEL)),
                  const((2 * PEER_HEADS, N_KEYS, N_KEYS)),
                  pl.BlockSpec((eblk, D_MODEL), lambda s: (jnp.minimum(s, nsteps - 1) % nblk, 0)),
                  pl.BlockSpec((eblk, D_MODEL), lambda s: (jnp.maximum(s - 2, 0) % nblk, 0))],
        out_specs=pl.BlockSpec((tt, D_MODEL), lambda s: (fin_tile(s), 0)),
        out_shape=jax.ShapeDtypeStruct((t, D_MODEL), F32),
        scratch_shapes=[pltpu.VMEM((D_MODEL, tt), BF16),
                        pltpu.VMEM((2 * PEER_HEADS * N_KEYS, tt), BF16),
                        pltpu.VMEM((2, nchunk, N_KEYS, LANES), F32),
                        stat(BF16), stat(BF16), stat(F32), stat(F32),
                        pltpu.VMEM((tt, D_MODEL), F32),
                        pltpu.VMEM((eblk, tt), F32), pltpu.VMEM((eblk, tt), F32),
                        pltpu.VMEM((eblk, tt), BF16), pltpu.VMEM((eblk, tt), BF16)],
        compiler_params=pltpu.CompilerParams(
            dimension_semantics=("arbitrary",), vmem_limit_bytes=VMEM_LIMIT),
        name="peer",
    )(x1, x1, sh2, sc2, g2, g_ffn, g_fin, wq_t, keys, u_tab, v_tab)


def kernel(x_prompt, x_sample, c_prompt, c_sample, cache_k, cache_v, state_conv, norm_mix_g, norm_ffn_g,
           w_ada, b_ada, w_in, conv_w, attn_sinks, w_out, peer_wq, peer_keys, peer_u, peer_v,
           norm_final_g):
    depth = w_ada.shape[0]
    assert depth == 1
    nb, seq, _ = x_prompt.shape
    ns, nt, _ = x_sample.shape
    assert nt == SAMPLE_T
    l = 0

    mod = _ada(jnp.concatenate([c_prompt, c_sample], axis=0), w_ada[l], b_ada[l])
    mod_p = jnp.transpose(mod[:, :nb], (1, 0, 2))
    mod_s = jnp.repeat(mod[:, nb:], nt, axis=1)

    g_mix = norm_mix_g[l].reshape(1, -1)
    g_ffn = norm_ffn_g[l].reshape(1, -1)
    g_fin = norm_final_g.reshape(1, -1)
    w_in_b = w_in[l].astype(BF16)
    w_out_b = w_out[l].astype(BF16)
    wq_t = peer_wq[l].T.astype(BF16)
    keys = peer_keys[l].reshape(2 * PEER_HEADS, N_KEYS, N_KEYS).astype(BF16)
    u_tab = peer_u[l].astype(BF16)
    v_tab = peer_v[l].astype(BF16)

    x1_p, k_p, v_p, conv_p = _mix_prompt(x_prompt, mod_p, g_mix, w_in_b, conv_w[l], attn_sinks[l], w_out_b)
    y_p = _peer(x1_p.reshape(nb * seq, D_MODEL), mod_p[:, 3:4], mod_p[:, 4:5], mod_p[:, 5:6],
                g_ffn, g_fin, wq_t, keys, u_tab, v_tab, rows_per_mod=seq)

    st = state_conv[l]
    zero = jnp.zeros_like(st[:, 0])
    s1 = jnp.stack([st[:, 1], zero, zero, zero], axis=1).reshape(ns * nt, D_CONV)
    s2 = jnp.stack([st[:, 0], st[:, 1], zero, zero], axis=1).reshape(ns * nt, D_CONV)
    x1_s, k_s, v_s, u_s = _mix_sample(
        x_sample.reshape(ns * nt, D_MODEL), mod_s[:N_ADA // 2], g_mix, w_in_b, conv_w[l], attn_sinks[l], w_out_b,
        s1, s2, cache_k[l].reshape(ns, WINDOW, KV_DIM), cache_v[l].reshape(ns, WINDOW, KV_DIM))
    y_s = _peer(x1_s, mod_s[3:4], mod_s[4:5], mod_s[5:6], g_ffn, g_fin, wq_t, keys, u_tab, v_tab,
                rows_per_mod=None)
    conv_s = u_s.reshape(ns, nt, D_CONV)[:, nt - (CONV_W - 1):]

    kv_shape = (1, -1, WINDOW, N_KV_HEADS, HEAD_DIM)
    return (y_p.reshape(nb, seq, D_MODEL), y_s.reshape(ns, nt, D_MODEL),
            k_p.reshape(kv_shape), v_p.reshape(kv_shape), conv_p[None],
            k_s.reshape(kv_shape), v_s.reshape(kv_shape), conv_s[None])
```

```python
import functools

import jax
import jax.numpy as jnp
import numpy as np
from jax import lax
from jax.experimental import pallas as pl
from jax.experimental.pallas import tpu as pltpu

F32 = jnp.float32
BF16 = jnp.bfloat16

D_MODEL = 1024
D_ATTN = 512
D_CONV = 512
HEAD_DIM = 64
N_HEADS = 8
N_KV_HEADS = 2
GROUP = 4
KV_DIM = 128
WINDOW = 128
CONV_W = 3
IN_COLS = D_ATTN + 2 * KV_DIM + 3 * D_CONV
COL_K = D_ATTN
COL_V = D_ATTN + KV_DIM
COL_B = D_ATTN + 2 * KV_DIM
COL_C = COL_B + D_CONV
COL_H = COL_C + D_CONV
N_ADA = 6
PEER_HEADS = 8
N_KEYS = 128
N_EXPERTS = N_KEYS * N_KEYS
TOPK = 16
EPS = 1e-6
NEG_INF = -1e30
SQRT_HALF = float(np.sqrt(0.5))

LANES = 128
SUBLANES = 8
MIX_TILE = 512
SAMPLE_SEQS = 16
PEER_TILE = 512
PEER_EBLK = 1024
BF16_ROWS = 16
VMEM_LIMIT = 56 * 1024 * 1024

SLOPES = tuple(float(2.0 ** (-8.0 * (h + 1) / N_HEADS)) for h in range(N_HEADS))


def _rms(x, g):
    return x * lax.rsqrt(jnp.mean(x * x, axis=-1, keepdims=True) + EPS) * g


def _per_group(rows_group, vals):
    out = jnp.full(rows_group.shape, vals[0], F32)
    for g in range(1, len(vals)):
        out = jnp.where(rows_group == g, vals[g], out)
    return out


def _ada_kernel(c_ref, w_ref, b_ref, o_ref):
    s = jax.nn.silu(c_ref[...])
    o_ref[...] = jnp.dot(s.astype(BF16), w_ref[...].astype(BF16),
                         preferred_element_type=F32) + b_ref[...]


def _ada(c, w_ada, b_ada):
    n = c.shape[0]
    return pl.pallas_call(
        _ada_kernel,
        grid=(N_ADA,),
        in_specs=[pl.BlockSpec((n, D_MODEL), lambda j: (0, 0)),
                  pl.BlockSpec((D_MODEL, D_MODEL), lambda j: (0, j)),
                  pl.BlockSpec((1, D_MODEL), lambda j: (0, j))],
        out_specs=pl.BlockSpec((None, n, D_MODEL), lambda j: (j, 0, 0)),
        out_shape=jax.ShapeDtypeStruct((N_ADA, n, D_MODEL), F32),
        name="ada",
    )(c, w_ada, b_ada.reshape(1, -1))


def _sink_softmax(s, sink):
    m = jnp.maximum(jnp.max(s, axis=-1, keepdims=True), sink)
    p = jnp.exp(s - m)
    den = jnp.sum(p, axis=-1, keepdims=True) + jnp.exp(sink - m)
    return p / den


def _mix_prompt_kernel(sink_ref, x_ref, mod_ref, g_ref, win_ref, cw_ref, wout_ref,
                       x1_ref, ko_ref, vo_ref, co_ref,
                       proj, zbuf, ubuf, kprev, vprev):
    j = pl.program_id(1)
    tm = x_ref.shape[0]
    nblk = tm // WINDOW

    @pl.when(j == 0)
    def _():
        kprev[...] = jnp.zeros_like(kprev)
        vprev[...] = jnp.zeros_like(vprev)
        ubuf[0:SUBLANES, :] = jnp.zeros((SUBLANES, D_CONV), F32)

    x = x_ref[...]
    h = _rms(x, g_ref[...]) * (1.0 + mod_ref[1:2, :]) + mod_ref[0:1, :]
    proj[...] = jnp.dot(h.astype(BF16), win_ref[...], preferred_element_type=F32)

    shp = (GROUP * WINDOW, 2 * WINDOW)
    row = lax.broadcasted_iota(jnp.int32, shp, 0)
    col = lax.broadcasted_iota(jnp.int32, shp, 1)
    dist = (row & (WINDOW - 1)) + WINDOW - col
    valid_any = (dist >= 0) & (dist < WINDOW)
    distf = dist.astype(F32)
    grp = lax.broadcasted_iota(jnp.int32, (GROUP * WINDOW, 1), 0) // WINDOW
    first_col = jnp.where(j == 0, WINDOW, 0)

    for kv in range(N_KV_HEADS):
        pen = _per_group(grp, SLOPES[kv * GROUP:(kv + 1) * GROUP]) * distf
        sink = _per_group(grp, [sink_ref[kv * GROUP + g] for g in range(GROUP)])
        ks = slice(kv * HEAD_DIM, (kv + 1) * HEAD_DIM)
        for b in range(nblk):
            rows = slice(b * WINDOW, (b + 1) * WINDOW)
            if b == 0:
                kp, vp = kprev[:, ks], vprev[:, ks]
                valid = valid_any & (col >= first_col)
            else:
                prows = slice((b - 1) * WINDOW, b * WINDOW)
                kp = proj[prows, COL_K + kv * HEAD_DIM:COL_K + (kv + 1) * HEAD_DIM]
                vp = proj[prows, COL_V + kv * HEAD_DIM:COL_V + (kv + 1) * HEAD_DIM]
                valid = valid_any
            kc = proj[rows, COL_K + kv * HEAD_DIM:COL_K + (kv + 1) * HEAD_DIM]
            vc = proj[rows, COL_V + kv * HEAD_DIM:COL_V + (kv + 1) * HEAD_DIM]
            q4 = jnp.concatenate(
                [proj[rows, (kv * GROUP + g) * HEAD_DIM:(kv * GROUP + g + 1) * HEAD_DIM]
                 for g in range(GROUP)], axis=0)
            kcat = jnp.concatenate([kp, kc], axis=0)
            vcat = jnp.concatenate([vp, vc], axis=0)
            s = lax.dot_general(q4.astype(BF16), kcat.astype(BF16), (((1,), (1,)), ((), ())),
                                preferred_element_type=F32) * (HEAD_DIM ** -0.5) - pen
            s = jnp.where(valid, s, NEG_INF)
            w = _sink_softmax(s, sink)
            o = jnp.dot(w.astype(BF16), vcat.astype(BF16), preferred_element_type=F32)
            for g in range(GROUP):
                hcol = (kv * GROUP + g) * HEAD_DIM
                zbuf[rows, hcol:hcol + HEAD_DIM] = o[g * WINDOW:(g + 1) * WINDOW, :]

    u = proj[:, COL_C:COL_C + D_CONV] * proj[:, COL_H:COL_H + D_CONV]
    ubuf[SUBLANES:SUBLANES + tm, :] = u
    off = SUBLANES - (CONV_W - 1)
    y = cw_ref[0:1, :] * ubuf[off:off + tm, :]
    for t in range(1, CONV_W):
        y = y + cw_ref[t:t + 1, :] * ubuf[off + t:off + t + tm, :]
    zbuf[:, D_ATTN:] = proj[:, COL_B:COL_B + D_CONV] * y
    ubuf[0:SUBLANES, :] = ubuf[tm:tm + SUBLANES, :]

    out = jnp.dot(zbuf[...].astype(BF16), wout_ref[...], preferred_element_type=F32)
    x1_ref[...] = x + mod_ref[2:3, :] * out

    kprev[...] = proj[tm - WINDOW:tm, COL_K:COL_K + KV_DIM]
    vprev[...] = proj[tm - WINDOW:tm, COL_V:COL_V + KV_DIM]

    @pl.when(j == pl.num_programs(1) - 1)
    def _():
        ko_ref[...] = proj[tm - WINDOW:tm, COL_K:COL_K + KV_DIM]
        vo_ref[...] = proj[tm - WINDOW:tm, COL_V:COL_V + KV_DIM]
        co_ref[...] = ubuf[SUBLANES - (CONV_W - 1):SUBLANES, :]


def _mix_prompt(x, mod, g_mix, w_in, conv_w, sinks, w_out):
    b, s, _ = x.shape
    tm = MIX_TILE
    const = lambda shape: pl.BlockSpec(shape, lambda i, j: (0,) * len(shape),
                                       pipeline_mode=pl.Buffered(1))
    return pl.pallas_call(
        _mix_prompt_kernel,
        grid=(b, s // tm),
        in_specs=[pl.BlockSpec(memory_space=pltpu.SMEM),
                  pl.BlockSpec((None, tm, D_MODEL), lambda i, j: (i, j, 0)),
                  pl.BlockSpec((None, N_ADA, D_MODEL), lambda i, j: (i, 0, 0)),
                  const((1, D_MODEL)),
                  const((D_MODEL, IN_COLS)),
                  const((CONV_W, D_CONV)),
                  const((D_MODEL, D_MODEL))],
        out_specs=[pl.BlockSpec((None, tm, D_MODEL), lambda i, j: (i, j, 0)),
                   pl.BlockSpec((None, WINDOW, KV_DIM), lambda i, j: (i, 0, 0)),
                   pl.BlockSpec((None, WINDOW, KV_DIM), lambda i, j: (i, 0, 0)),
                   pl.BlockSpec((None, CONV_W - 1, D_CONV), lambda i, j: (i, 0, 0))],
        out_shape=[jax.ShapeDtypeStruct((b, s, D_MODEL), F32),
                   jax.ShapeDtypeStruct((b, WINDOW, KV_DIM), F32),
                   jax.ShapeDtypeStruct((b, WINDOW, KV_DIM), F32),
                   jax.ShapeDtypeStruct((b, CONV_W - 1, D_CONV), F32)],
        scratch_shapes=[pltpu.VMEM((tm, IN_COLS), F32),
                        pltpu.VMEM((tm, D_MODEL), F32),
                        pltpu.VMEM((tm + SUBLANES, D_CONV), F32),
                        pltpu.VMEM((WINDOW, KV_DIM), F32),
                        pltpu.VMEM((WINDOW, KV_DIM), F32)],
        compiler_params=pltpu.CompilerParams(
            dimension_semantics=("arbitrary", "arbitrary"), vmem_limit_bytes=VMEM_LIMIT),
        name="mix_prompt",
    )(sinks, x, mod, g_mix, w_in, conv_w, w_out)


SAMPLE_T = 4
KALL = WINDOW + 2 * SUBLANES


def _mix_sample_kernel(sink_ref, x_ref, mod_ref, g_ref, win_ref, cw_ref, wout_ref,
                       s1_ref, s2_ref, ck_ref, cv_ref,
                       x1_ref, ko_ref, vo_ref, u_ref,
                       proj, zbuf, kall, vall):
    i = pl.program_id(0)
    nseq = ck_ref.shape[0]
    ntok = x_ref.shape[0]

    @pl.when(i == 0)
    def _():
        x = x_ref[...]
        h = _rms(x, g_ref[...]) * (1.0 + mod_ref[1]) + mod_ref[0]
        proj[...] = jnp.dot(h.astype(BF16), win_ref[...], preferred_element_type=F32)
        u = proj[:, COL_C:COL_C + D_CONV] * proj[:, COL_H:COL_H + D_CONV]
        t_of = lax.broadcasted_iota(jnp.int32, (ntok, 1), 0) & (SAMPLE_T - 1)
        u1 = jnp.where(t_of >= 1, pltpu.roll(u, 1, 0), 0.0) + s1_ref[...]
        u2 = jnp.where(t_of >= 2, pltpu.roll(u, 2, 0), 0.0) + s2_ref[...]
        y = cw_ref[0:1, :] * u2 + cw_ref[1:2, :] * u1 + cw_ref[2:3, :] * u
        zbuf[:, D_ATTN:] = proj[:, COL_B:COL_B + D_CONV] * y
        u_ref[...] = u
        for a in (kall, vall):
            a[:, WINDOW + SUBLANES:, :] = jnp.zeros((2, SUBLANES, KV_DIM), F32)

    nrow = GROUP * 2 * SAMPLE_T
    shp = (nrow, KALL)
    row = lax.broadcasted_iota(jnp.int32, shp, 0)
    col = lax.broadcasted_iota(jnp.int32, shp, 1)
    t_row = row & (SAMPLE_T - 1)
    seq_row = (row >> 2) & 1
    new_c = col - WINDOW
    is_cache = col < WINDOW
    dist = jnp.where(is_cache, WINDOW + t_row - col, t_row - (new_c & (SAMPLE_T - 1)))
    valid = ((is_cache & (col > t_row))
             | ((new_c >= 0) & (new_c < 2 * SAMPLE_T) & ((new_c >> 2) == seq_row) & (dist >= 0)))
    distf = dist.astype(F32)
    grp = lax.broadcasted_iota(jnp.int32, (nrow, 1), 0) // (2 * SAMPLE_T)
    pens = [_per_group(grp, SLOPES[kv * GROUP:(kv + 1) * GROUP]) * distf for kv in range(N_KV_HEADS)]
    sinkc = [_per_group(grp, [sink_ref[kv * GROUP + g] for g in range(GROUP)])
             for kv in range(N_KV_HEADS)]
    seq0 = seq_row == 0

    def pair(p, carry):
        r0 = pl.multiple_of(i * (nseq * SAMPLE_T) + p * 2 * SAMPLE_T, 2 * SAMPLE_T)
        rows = pl.ds(r0, 2 * SAMPLE_T)
        knew = proj[rows, COL_K:COL_K + KV_DIM]
        vnew = proj[rows, COL_V:COL_V + KV_DIM]
        for sq in range(2):
            n = 2 * p + sq
            ck = ck_ref[n]
            cv = cv_ref[n]
            kall[sq, 0:WINDOW, :] = ck
            vall[sq, 0:WINDOW, :] = cv
            kall[sq, WINDOW:WINDOW + SUBLANES, :] = knew
            vall[sq, WINDOW:WINDOW + SUBLANES, :] = vnew
            ko_ref[n, 0:WINDOW - SAMPLE_T, :] = ck[SAMPLE_T:, :]
            vo_ref[n, 0:WINDOW - SAMPLE_T, :] = cv[SAMPLE_T:, :]
            ko_ref[n, WINDOW - SAMPLE_T:, :] = knew[sq * SAMPLE_T:(sq + 1) * SAMPLE_T, :]
            vo_ref[n, WINDOW - SAMPLE_T:, :] = vnew[sq * SAMPLE_T:(sq + 1) * SAMPLE_T, :]
        for kv in range(N_KV_HEADS):
            ks = slice(kv * HEAD_DIM, (kv + 1) * HEAD_DIM)
            q = jnp.concatenate(
                [proj[rows, (kv * GROUP + g) * HEAD_DIM:(kv * GROUP + g + 1) * HEAD_DIM]
                 for g in range(GROUP)], axis=0).astype(BF16)
            sc = [lax.dot_general(q, kall[sq, :, ks].astype(BF16), (((1,), (1,)), ((), ())),
                                  preferred_element_type=F32) for sq in range(2)]
            s = jnp.where(seq0, sc[0], sc[1]) * (HEAD_DIM ** -0.5) - pens[kv]
            s = jnp.where(valid, s, NEG_INF)
            w = _sink_softmax(s, sinkc[kv])
            o = (jnp.dot(jnp.where(seq0, w, 0.0).astype(BF16), vall[0, :, ks].astype(BF16),
                         preferred_element_type=F32)
                 + jnp.dot(jnp.where(seq0, 0.0, w).astype(BF16), vall[1, :, ks].astype(BF16),
                           preferred_element_type=F32))
            for g in range(GROUP):
                hcol = (kv * GROUP + g) * HEAD_DIM
                zbuf[rows, hcol:hcol + HEAD_DIM] = o[g * 2 * SAMPLE_T:(g + 1) * 2 * SAMPLE_T, :]
        return carry

    lax.fori_loop(0, nseq // 2, pair, 0)

    @pl.when(i == pl.num_programs(0) - 1)
    def _():
        out = jnp.dot(zbuf[...].astype(BF16), wout_ref[...], preferred_element_type=F32)
        x1_ref[...] = x_ref[...] + mod_ref[2] * out


def _mix_sample(x, mod_tok, g_mix, w_in, conv_w, sinks, w_out, s1, s2, cache_k, cache_v):
    ntok = x.shape[0]
    nseq_all = cache_k.shape[0]
    ns = SAMPLE_SEQS
    const = lambda shape: pl.BlockSpec(shape, lambda i: (0,) * len(shape),
                                       pipeline_mode=pl.Buffered(1))
    return pl.pallas_call(
        _mix_sample_kernel,
        grid=(nseq_all // ns,),
        in_specs=[pl.BlockSpec(memory_space=pltpu.SMEM),
                  const((ntok, D_MODEL)),
                  const((N_ADA // 2, ntok, D_MODEL)),
                  const((1, D_MODEL)),
                  const((D_MODEL, IN_COLS)),
                  const((CONV_W, D_CONV)),
                  const((D_MODEL, D_MODEL)),
                  const((ntok, D_CONV)),
                  const((ntok, D_CONV)),
                  pl.BlockSpec((ns, WINDOW, KV_DIM), lambda i: (i, 0, 0)),
                  pl.BlockSpec((ns, WINDOW, KV_DIM), lambda i: (i, 0, 0))],
        out_specs=[const((ntok, D_MODEL)),
                   pl.BlockSpec((ns, WINDOW, KV_DIM), lambda i: (i, 0, 0)),
                   pl.BlockSpec((ns, WINDOW, KV_DIM), lambda i: (i, 0, 0)),
                   const((ntok, D_CONV))],
        out_shape=[jax.ShapeDtypeStruct((ntok, D_MODEL), F32),
                   jax.ShapeDtypeStruct(cache_k.shape, F32),
                   jax.ShapeDtypeStruct(cache_v.shape, F32),
                   jax.ShapeDtypeStruct((ntok, D_CONV), F32)],
        scratch_shapes=[pltpu.VMEM((ntok, IN_COLS), F32),
                        pltpu.VMEM((ntok, D_MODEL), F32),
                        pltpu.VMEM((2, KALL, KV_DIM), F32),
                        pltpu.VMEM((2, KALL, KV_DIM), F32)],
        compiler_params=pltpu.CompilerParams(
            dimension_semantics=("arbitrary",), vmem_limit_bytes=VMEM_LIMIT),
        name="mix_sample",
    )(sinks, x, mod_tok, g_mix, w_in, conv_w, w_out, s1, s2, cache_k, cache_v)


CAND_COLS = (16, 8, 5, 4, 3, 2, 2, 2)
CAND_ROWS = 16 + SUBLANES * 7 + SUBLANES


def _pop_max(work, tie_break):
    m = jnp.max(work, axis=0, keepdims=True)
    sel = work == m
    if tie_break:
        iota = lax.broadcasted_iota(jnp.int32, work.shape, 0).astype(F32)
        idx = jnp.min(jnp.where(sel, iota, float(work.shape[0])), axis=0, keepdims=True)
        sel = iota == idx
    return m, sel


def _top16(s, tie_break):
    i16 = lax.broadcasted_iota(jnp.int32, (TOPK, s.shape[1]), 0)
    work = s
    rank = jnp.full(s.shape, float(TOPK), F32)
    vals = jnp.zeros((TOPK, s.shape[1]), F32)
    for r in range(TOPK):
        m, sel = _pop_max(work, tie_break)
        rank = jnp.where(sel, float(r), rank)
        work = jnp.where(sel, -jnp.inf, work)
        vals = jnp.where(i16 == r, m, vals)
    taken = jnp.sum(jnp.where(rank < float(TOPK), 1.0, 0.0), axis=0, keepdims=True)
    return rank, vals, taken


def _cand_layout(a, b, combine, fill):
    i8 = lax.broadcasted_iota(jnp.int32, (SUBLANES, a.shape[1]), 0)
    pieces = [combine(a[0:1, :], b)]
    for r in range(1, SUBLANES):
        pieces.append(jnp.where(i8 < CAND_COLS[r], combine(a[r:r + 1, :], b[0:SUBLANES, :]), fill))
    pieces.append(combine(a[SUBLANES:, :], b[0:1, :]))
    return jnp.concatenate(pieces, axis=0)


def _select_products(a, b, tie_break):
    lanes = a.shape[1]
    cand = _cand_layout(a, b, lambda x, y: x + y, -jnp.inf)
    work = cand
    taken = jnp.zeros(cand.shape, F32)
    for _ in range(TOPK):
        _, sel = _pop_max(work, tie_break)
        taken = jnp.where(sel, 1.0, taken)
        work = jnp.where(sel, -jnp.inf, work)
    ea = jnp.exp(a - a[0:1, :])
    eb = jnp.exp(b - b[0:1, :])
    z = jnp.sum(taken * _cand_layout(ea, eb, lambda x, y: x * y, 0.0), axis=0, keepdims=True)
    i8 = lax.broadcasted_iota(jnp.int32, (SUBLANES, lanes), 0)
    low = jnp.zeros((SUBLANES, lanes), F32)
    low = jnp.where(i8 == 0, jnp.sum(taken[0:TOPK, :], axis=0, keepdims=True), low)
    for r in range(1, SUBLANES):
        base = TOPK + SUBLANES * (r - 1)
        low = jnp.where(i8 == r, jnp.sum(taken[base:base + SUBLANES, :], axis=0, keepdims=True), low)
    ncol = jnp.concatenate([low, taken[CAND_ROWS - SUBLANES:, :]], axis=0)
    return ncol, z, jnp.sum(ncol, axis=0, keepdims=True)


def _peer_kernel(xsel_ref, xfin_ref, sh_ref, sc_ref, gt_ref, gffn_ref, gfin_ref, wqt_ref, keys_ref,
                 u_ref, vt_ref,
                 y_ref,
                 h2t, qt, sbuf, rk1, e1, nsel, e0n, acc, act0, act1, w0, w1, *, ntiles):
    s = pl.program_id(0)
    tt = xsel_ref.shape[0]
    nchunk = tt // LANES
    eblk = u_ref.shape[0]
    nblk = N_EXPERTS // eblk
    nsteps = ntiles * nblk
    act, wbuf = (act0, act1), (w0, w1)
    new_tile = (s % nblk == 0) & (s < nsteps)

    @pl.when(s == 0)
    def _():
        for ref in (rk1, e1, nsel, e0n, acc, act0, act1, w0, w1):
            ref[...] = jnp.zeros(ref.shape, ref.dtype)

    @pl.when(new_tile)
    def _():
        h2 = _rms(xsel_ref[...], gffn_ref[...]) * (1.0 + sc_ref[...]) + sh_ref[...]
        h2t[...] = h2.T.astype(BF16)

    def retrieval():
        qt[...] = jnp.dot(wqt_ref[...], h2t[...], preferred_element_type=F32).astype(BF16)

        def head(h, carry):
            for p in range(2):
                r0 = pl.multiple_of((2 * h + p) * N_KEYS, N_KEYS)
                s = jnp.dot(keys_ref[2 * h + p], qt[pl.ds(r0, N_KEYS), :],
                            preferred_element_type=F32)
                for c in range(nchunk):
                    sbuf[p, c] = s[:, c * LANES:(c + 1) * LANES]

            def select(c, tie_break):
                s0 = sbuf[0, c]
                s1 = sbuf[1, c]
                rank0, a, n0 = _top16(s0, tie_break)
                rank1, b, n1 = _top16(s1, tie_break)
                ncol, z, n2 = _select_products(a, b, tie_break)
                ns = jnp.zeros(s0.shape, F32)
                for r in range(TOPK):
                    ns = jnp.where(rank0 == float(r), ncol[r:r + 1, :], ns)
                nsel[c, h] = ns
                rk1[c, h] = rank1.astype(BF16)
                e0n[c, h] = jnp.exp(s0 - a[0:1, :]) / z
                e1[c, h] = jnp.exp(s1 - b[0:1, :]).astype(BF16)
                return jnp.max(jnp.maximum(jnp.maximum(n0, n1), n2))

            def chunk(c, carry2):
                most_taken = select(c, tie_break=False)

                @pl.when(most_taken > float(TOPK))
                def _():
                    select(c, tie_break=True)

                return carry2

            lax.fori_loop(0, nchunk, chunk, 0)
            return carry

        lax.fori_loop(0, PEER_HEADS, head, 0)

    i_base = (jnp.maximum(s - 1, 0) % nblk) * (eblk // N_KEYS)

    def stages(p):
        act[p][...] = jnp.dot(u_ref[...], h2t[...], preferred_element_type=F32)

        for ig in range(eblk // N_KEYS):
            i_glob = i_base + ig
            for c in range(nchunk):
                lanes = slice(c * LANES, (c + 1) * LANES)
                g = [None] * (N_KEYS // BF16_ROWS)
                for h in range(PEER_HEADS):
                    ns = jnp.broadcast_to(nsel[c, h, pl.ds(i_glob, 1), :],
                                          (BF16_ROWS, LANES)).astype(BF16)
                    e0 = jnp.broadcast_to(e0n[c, h, pl.ds(i_glob, 1), :],
                                          (BF16_ROWS, LANES)).astype(BF16)
                    for jg in range(N_KEYS // BF16_ROWS):
                        r = slice(jg * BF16_ROWS, (jg + 1) * BF16_ROWS)
                        t = jnp.where(rk1[c, h, r, :] < ns, e1[c, h, r, :] * e0, 0)
                        g[jg] = t if h == 0 else g[jg] + t
                for jg in range(N_KEYS // BF16_ROWS):
                    rows = slice(ig * N_KEYS + jg * BF16_ROWS, ig * N_KEYS + (jg + 1) * BF16_ROWS)
                    a = act[1 - p][rows, lanes].astype(BF16)
                    gelu = 0.5 * a * (1 + lax.erf(a * SQRT_HALF))
                    wbuf[p][rows, lanes] = g[jg] * gelu

        acc[...] += jnp.dot(vt_ref[...], wbuf[1 - p][...], preferred_element_type=F32)

    for p in range(2):
        pl.when(s % 2 == p)(functools.partial(stages, p))

    pl.when(new_tile)(retrieval)

    @pl.when((s >= 2) & ((s - 2) % nblk == nblk - 1))
    def _():
        x2 = xfin_ref[...] + gt_ref[...] * acc[...].T
        y_ref[...] = _rms(x2, gfin_ref[...])
        acc[...] = jnp.zeros_like(acc)


def _peer(x1, sh2, sc2, g2, g_ffn, g_fin, wq_t, keys, u_tab, v_tab, rows_per_mod):
    t = x1.shape[0]
    tt = PEER_TILE
    eblk = PEER_EBLK
    nchunk = tt // LANES
    ntiles = t // tt
    nblk = N_EXPERTS // eblk
    nsteps = ntiles * nblk
    sel_tile = lambda s: jnp.minimum(s // nblk, ntiles - 1)
    fin_tile = lambda s: jnp.maximum(s - 2, 0) // nblk
    once = pl.Buffered(1)
    if rows_per_mod is None:
        mod_spec = lambda tile: pl.BlockSpec((None, tt, D_MODEL), lambda s: (0, tile(s), 0),
                                             pipeline_mode=once)
    else:
        per = rows_per_mod // tt
        mod_spec = lambda tile: pl.BlockSpec((None, 1, D_MODEL), lambda s: (tile(s) // per, 0, 0),
                                             pipeline_mode=once)
    const = lambda shape: pl.BlockSpec(shape, lambda s: (0,) * len(shape), pipeline_mode=once)
    stat = lambda dt: pltpu.VMEM((nchunk, PEER_HEADS, N_KEYS, LANES), dt)
    return pl.pallas_call(
        functools.partial(_peer_kernel, ntiles=ntiles),
        grid=(nsteps + 2,),
        in_specs=[pl.BlockSpec((tt, D_MODEL), lambda s: (sel_tile(s), 0), pipeline_mode=once),
                  pl.BlockSpec((tt, D_MODEL), lambda s: (fin_tile(s), 0), pipeline_mode=once),
                  mod_spec(sel_tile), mod_spec(sel_tile), mod_spec(fin_tile),
                  const((1, D_MODEL)), const((1, D_MODEL)),
                  const((2 * PEER_HEADS * N_KEYS, D_MODEL)),
                  const((2 * PEER_HEADS, N_KEYS, N_KEYS)),
                  pl.BlockSpec((eblk, D_MODEL), lambda s: (jnp.minimum(s, nsteps - 1) % nblk, 0)),
                  pl.BlockSpec((D_MODEL, eblk), lambda s: (0, jnp.maximum(s - 2, 0) % nblk))],
        out_specs=pl.BlockSpec((tt, D_MODEL), lambda s: (fin_tile(s), 0)),
        out_shape=jax.ShapeDtypeStruct((t, D_MODEL), F32),
        scratch_shapes=[pltpu.VMEM((D_MODEL, tt), BF16),
                        pltpu.VMEM((2 * PEER_HEADS * N_KEYS, tt), BF16),
                        pltpu.VMEM((2, nchunk, N_KEYS, LANES), F32),
                        stat(BF16), stat(BF16), stat(F32), stat(F32),
                        pltpu.VMEM((D_MODEL, tt), F32),
                        pltpu.VMEM((eblk, tt), F32), pltpu.VMEM((eblk, tt), F32),
                        pltpu.VMEM((eblk, tt), BF16), pltpu.VMEM((eblk, tt), BF16)],
        compiler_params=pltpu.CompilerParams(
            dimension_semantics=("arbitrary",), vmem_limit_bytes=VMEM_LIMIT),
        name="peer",
    )(x1, x1, sh2, sc2, g2, g_ffn, g_fin, wq_t, keys, u_tab, v_tab)


def kernel(x_prompt, x_sample, c_prompt, c_sample, cache_k, cache_v, state_conv, norm_mix_g, norm_ffn_g,
           w_ada, b_ada, w_in, conv_w, attn_sinks, w_out, peer_wq, peer_keys, peer_u, peer_v,
           norm_final_g):
    depth = w_ada.shape[0]
    assert depth == 1
    nb, seq, _ = x_prompt.shape
    ns, nt, _ = x_sample.shape
    assert nt == SAMPLE_T
    l = 0

    mod = _ada(jnp.concatenate([c_prompt, c_sample], axis=0), w_ada[l], b_ada[l])
    mod_p = jnp.transpose(mod[:, :nb], (1, 0, 2))
    mod_s = jnp.repeat(mod[:, nb:], nt, axis=1)

    g_mix = norm_mix_g[l].reshape(1, -1)
    g_ffn = norm_ffn_g[l].reshape(1, -1)
    g_fin = norm_final_g.reshape(1, -1)
    w_in_b = w_in[l].astype(BF16)
    w_out_b = w_out[l].astype(BF16)
    wq_t = peer_wq[l].T.astype(BF16)
    keys = peer_keys[l].reshape(2 * PEER_HEADS, N_KEYS, N_KEYS).astype(BF16)
    u_tab = peer_u[l].astype(BF16)
    v_tab = peer_v[l].T.astype(BF16)

    x1_p, k_p, v_p, conv_p = _mix_prompt(x_prompt, mod_p, g_mix, w_in_b, conv_w[l], attn_sinks[l], w_out_b)
    y_p = _peer(x1_p.reshape(nb * seq, D_MODEL), mod_p[:, 3:4], mod_p[:, 4:5], mod_p[:, 5:6],
                g_ffn, g_fin, wq_t, keys, u_tab, v_tab, rows_per_mod=seq)

    st = state_conv[l]
    zero = jnp.zeros_like(st[:, 0])
    s1 = jnp.stack([st[:, 1], zero, zero, zero], axis=1).reshape(ns * nt, D_CONV)
    s2 = jnp.stack([st[:, 0], st[:, 1], zero, zero], axis=1).reshape(ns * nt, D_CONV)
    x1_s, k_s, v_s, u_s = _mix_sample(
        x_sample.reshape(ns * nt, D_MODEL), mod_s[:N_ADA // 2], g_mix, w_in_b, conv_w[l], attn_sinks[l], w_out_b,
        s1, s2, cache_k[l].reshape(ns, WINDOW, KV_DIM), cache_v[l].reshape(ns, WINDOW, KV_DIM))
    y_s = _peer(x1_s, mod_s[3:4], mod_s[4:5], mod_s[5:6], g_ffn, g_fin, wq_t, keys, u_tab, v_tab,
                rows_per_mod=None)
    conv_s = u_s.reshape(ns, nt, D_CONV)[:, nt - (CONV_W - 1):]

    kv_shape = (1, -1, WINDOW, N_KV_HEADS, HEAD_DIM)
    return (y_p.reshape(nb, seq, D_MODEL), y_s.reshape(ns, nt, D_MODEL),
            k_p.reshape(kv_shape), v_p.reshape(kv_shape), conv_p[None],
            k_s.reshape(kv_shape), v_s.reshape(kv_shape), conv_s[None])
```
